```python
import jax
import jax.numpy as jnp
from jax import lax
import numpy as np

D_MODEL = 2048
BATCH = 2
SEQ = 16384
DEPTH = 2

N_EVEN = (DEPTH + 1) // 2
N_ODD = DEPTH // 2

D_FF = 5632
FFN_HALF = 0.5
NORM_EPS = 1e-6

D_CONV = D_MODEL // 2
CONV_WIDTH = 3
HEAD_DIM = 64
N_Q_HEADS = (D_MODEL // 2) // HEAD_DIM
N_KV_HEADS = 4
GQA_GROUP = N_Q_HEADS // N_KV_HEADS
WINDOW = 128
ATTN_BLOCK = 128
D_ATTN = N_Q_HEADS * HEAD_DIM
D_KV = N_KV_HEADS * HEAD_DIM
D_IN_EVEN = 3 * D_CONV + D_ATTN + 2 * D_KV
D_MIX_EVEN = D_CONV + D_ATTN

GLA_HEADS = 4
GLA_DK = D_MODEL // 2
GLA_DV = D_MODEL
GLA_HK = GLA_DK // GLA_HEADS
GLA_HV = GLA_DV // GLA_HEADS
GLA_GATE_RANK = 16
GLA_GATE_NORMALIZER = 16.0
GLA_CHUNK = 64
D_IN_ODD = 2 * GLA_DK + 2 * GLA_DV + GLA_GATE_RANK

kernel_name = 'hybrid_macaron_conv_swa_gla'


def rms_norm(x, g):
    xf = x.astype(jnp.float32)
    y = xf * lax.rsqrt(jnp.mean(xf * xf, axis=-1, keepdims=True) + NORM_EPS)
    return (y * g.astype(jnp.float32)).astype(x.dtype)


def swiglu(x, w1, w3, w2):
    return (jax.nn.silu(x @ w1) * (x @ w3)) @ w2


def _alibi_slopes():
    ex = np.arange(1, N_Q_HEADS + 1, dtype=np.float32) * np.float32(8.0 / N_Q_HEADS)
    return jnp.asarray(np.power(np.float32(2.0), -ex), dtype=jnp.float32)


def short_conv_mixer(u_b, u_c, u_x, conv_w):
    h = u_c * u_x
    y = lax.conv_general_dilated(
        h, conv_w.astype(h.dtype)[:, None, :], window_strides=(1,),
        padding=[(CONV_WIDTH - 1, 0)], dimension_numbers=('NWC', 'WIO', 'NWC'),
        feature_group_count=D_CONV)
    return u_b * y


def sliding_window_attention(q, k, v, sinks):
    bsz, seq = q.shape[0], q.shape[1]
    nb = seq // ATTN_BLOCK
    qb = q.reshape(bsz, nb, ATTN_BLOCK, N_KV_HEADS, GQA_GROUP, HEAD_DIM)

    def key_blocks(t):
        tp = jnp.pad(t, ((0, 0), (ATTN_BLOCK, 0), (0, 0), (0, 0)))
        tp = tp.reshape(bsz, nb + 1, ATTN_BLOCK, N_KV_HEADS, HEAD_DIM)
        return jnp.concatenate([tp[:, :-1], tp[:, 1:]], axis=2)

    kb, vb = key_blocks(k), key_blocks(v)
    scores = jnp.einsum('bnqhgd,bnkhd->bnhgqk', qb, kb,
                        preferred_element_type=jnp.float32) * (HEAD_DIM ** -0.5)
    blk = jnp.arange(nb)[:, None] * ATTN_BLOCK
    q_pos = blk + jnp.arange(ATTN_BLOCK)[None, :]
    k_pos = blk - ATTN_BLOCK + jnp.arange(2 * ATTN_BLOCK)[None, :]
    dist = q_pos[:, :, None] - k_pos[:, None, :]
    valid = (dist >= 0) & (dist < WINDOW) & (k_pos[:, None, :] >= 0)
    slopes = _alibi_slopes().reshape(1, 1, N_KV_HEADS, GQA_GROUP, 1, 1)
    dist_f = dist.astype(jnp.float32)[None, :, None, None]
    scores = jnp.where(valid[None, :, None, None], scores - slopes * dist_f, -jnp.inf)
    sink = sinks.astype(jnp.float32).reshape(1, 1, N_KV_HEADS, GQA_GROUP, 1)
    m = jnp.maximum(jnp.max(scores, axis=-1), sink)
    e = jnp.exp(scores - m[..., None])
    denom = jnp.sum(e, axis=-1) + jnp.exp(sink - m)
    p = (e / denom[..., None]).astype(v.dtype)
    out = jnp.einsum('bnhgqk,bnkhd->bnqhgd', p, vb)
    return out.reshape(bsz, seq, D_ATTN)


def gla_chunked(q, k, v, log_a):
    bsz, seq = q.shape[0], q.shape[1]
    nc = seq // GLA_CHUNK
    f32 = jnp.float32

    def chunk(t):
        return t.reshape(bsz, nc, GLA_CHUNK, GLA_HEADS, -1).transpose(0, 1, 3, 2, 4).astype(f32)

    qc = chunk(q) * (GLA_HK ** -0.5)
    kc, vc, gc = chunk(k), chunk(v), chunk(log_a)
    b = jnp.cumsum(gc, axis=3)
    b_last = b[:, :, :, -1:, :]
    q_dec = qc * jnp.exp(b)
    k_inv = kc * jnp.exp(-b)
    k_tail = kc * jnp.exp(b_last - b)
    causal = jnp.tril(jnp.ones((GLA_CHUNK, GLA_CHUNK), dtype=bool))
    attn = jnp.where(causal, jnp.einsum('bnhid,bnhjd->bnhij', q_dec, k_inv), 0.0)
    o_intra = jnp.einsum('bnhij,bnhjv->bnhiv', attn, vc)

    def step(state, inp):
        q_t, k_t, v_t, d_t = inp
        o_t = jnp.einsum('bhid,bhdv->bhiv', q_t, state)
        state = d_t[..., None] * state + jnp.einsum('bhjd,bhjv->bhdv', k_t, v_t)
        return state, o_t

    s0 = jnp.zeros((bsz, GLA_HEADS, GLA_HK, GLA_HV), f32)
    xs = (jnp.moveaxis(q_dec, 1, 0), jnp.moveaxis(k_tail, 1, 0),
          jnp.moveaxis(vc, 1, 0), jnp.moveaxis(jnp.exp(b_last[:, :, :, 0]), 1, 0))
    _, o_inter = lax.scan(step, s0, xs)
    o = o_intra + jnp.moveaxis(o_inter, 0, 1)
    return o.transpose(0, 1, 3, 2, 4).reshape(bsz, seq, GLA_HEADS, GLA_HV)


def even_mixer(h, w_in, conv_w, sinks, w_out):
    bsz, seq = h.shape[0], h.shape[1]
    u = h @ w_in
    cuts = [D_CONV, 2 * D_CONV, 3 * D_CONV, 3 * D_CONV + D_ATTN, 3 * D_CONV + D_ATTN + D_KV]
    u_b, u_c, u_x, q, k, v = jnp.split(u, cuts, axis=-1)
    y_conv = short_conv_mixer(u_b, u_c, u_x, conv_w)
    y_attn = sliding_window_attention(
        q.reshape(bsz, seq, N_Q_HEADS, HEAD_DIM),
        k.reshape(bsz, seq, N_KV_HEADS, HEAD_DIM),
        v.reshape(bsz, seq, N_KV_HEADS, HEAD_DIM), sinks)
    return jnp.concatenate([y_conv, y_attn.astype(y_conv.dtype)], axis=-1) @ w_out


def odd_mixer(h, w_in, w_gate_up, b_gate_up, head_g, w_out):
    bsz, seq = h.shape[0], h.shape[1]
    u = h @ w_in
    cuts = [GLA_DK, 2 * GLA_DK, 2 * GLA_DK + GLA_DV, 2 * GLA_DK + 2 * GLA_DV]
    q, k, v, r, g_low = jnp.split(u, cuts, axis=-1)
    log_a = jax.nn.log_sigmoid((g_low @ w_gate_up + b_gate_up).astype(jnp.float32)) / GLA_GATE_NORMALIZER
    o = gla_chunked(q.reshape(bsz, seq, GLA_HEADS, GLA_HK),
                    k.reshape(bsz, seq, GLA_HEADS, GLA_HK),
                    v.reshape(bsz, seq, GLA_HEADS, GLA_HV),
                    log_a.reshape(bsz, seq, GLA_HEADS, GLA_HK))
    o = rms_norm(o, head_g).reshape(bsz, seq, GLA_DV)
    o = o * jax.nn.silu(r.astype(jnp.float32))
    return o.astype(h.dtype) @ w_out


def setup_inputs(seed: int = 0) -> dict:
    key = jax.random.key(seed)
    ks = jax.random.split(key, 18)
    f32 = jnp.float32

    def w(k, shape, fan_in):
        return jax.random.normal(k, shape, f32) * (fan_in ** -0.5)

    return {
        'x': jax.random.normal(ks[0], (BATCH, SEQ, D_MODEL), f32),
        'norm_g': 1.0 + 0.02 * jax.random.normal(ks[1], (DEPTH, 3, D_MODEL), f32),
        'ffn_pre_w1': w(ks[2], (DEPTH, D_MODEL, D_FF), D_MODEL),
        'ffn_pre_w3': w(ks[3], (DEPTH, D_MODEL, D_FF), D_MODEL),
        'ffn_pre_w2': w(ks[4], (DEPTH, D_FF, D_MODEL), D_FF),
        'ffn_post_w1': w(ks[5], (DEPTH, D_MODEL, D_FF), D_MODEL),
        'ffn_post_w3': w(ks[6], (DEPTH, D_MODEL, D_FF), D_MODEL),
        'ffn_post_w2': w(ks[7], (DEPTH, D_FF, D_MODEL), D_FF),
        'even_w_in': w(ks[8], (N_EVEN, D_MODEL, D_IN_EVEN), D_MODEL),
        'even_conv_w': w(ks[9], (N_EVEN, CONV_WIDTH, D_CONV), CONV_WIDTH),
        'even_sinks': jax.random.normal(ks[10], (N_EVEN, N_Q_HEADS), f32),
        'even_w_out': w(ks[11], (N_EVEN, D_MIX_EVEN, D_MODEL), D_MIX_EVEN),
        'odd_w_in': w(ks[12], (N_ODD, D_MODEL, D_IN_ODD), D_MODEL),
        'odd_w_gate_up': w(ks[13], (N_ODD, GLA_GATE_RANK, GLA_DK), GLA_GATE_RANK),
        'odd_b_gate_up': 0.1 * jax.random.normal(ks[14], (N_ODD, GLA_DK), f32),
        'odd_head_g': 1.0 + 0.02 * jax.random.normal(ks[15], (N_ODD, GLA_HV), f32),
        'odd_w_out': w(ks[16], (N_ODD, GLA_DV, D_MODEL), GLA_DV),
        'final_g': 1.0 + 0.02 * jax.random.normal(ks[17], (D_MODEL,), f32),
    }


def reference(x, norm_g, ffn_pre_w1, ffn_pre_w3, ffn_pre_w2, ffn_post_w1, ffn_post_w3,
              ffn_post_w2, even_w_in, even_conv_w, even_sinks, even_w_out, odd_w_in,
              odd_w_gate_up, odd_b_gate_up, odd_head_g, odd_w_out, final_g):
    for layer in range(DEPTH):
        x = x + FFN_HALF * swiglu(rms_norm(x, norm_g[layer, 0]),
                                  ffn_pre_w1[layer], ffn_pre_w3[layer], ffn_pre_w2[layer])
        h = rms_norm(x, norm_g[layer, 1])
        if layer % 2 == 0:
            i = layer // 2
            x = x + even_mixer(h, even_w_in[i], even_conv_w[i], even_sinks[i], even_w_out[i])
        else:
            i = layer // 2
            x = x + odd_mixer(h, odd_w_in[i], odd_w_gate_up[i], odd_b_gate_up[i],
                              odd_head_g[i], odd_w_out[i])
        x = x + FFN_HALF * swiglu(rms_norm(x, norm_g[layer, 2]),
                                  ffn_post_w1[layer], ffn_post_w3[layer], ffn_post_w2[layer])
    return rms_norm(x, final_g)
```

```python
import functools

import jax
import jax.numpy as jnp
import numpy as np
from jax import lax
from jax.experimental import pallas as pl
from jax.experimental.pallas import tpu as pltpu

F32 = jnp.float32
MXU_DTYPE = jnp.bfloat16

NORM_EPS = 1e-6
FFN_HALF = 0.5
CONV_WIDTH = 3
HEAD_DIM = 64
N_KV_HEADS = 4
WINDOW = 128
ATTN_BLOCK = 128
GLA_HEADS = 4
GLA_CHUNK = 64
GLA_GATE_NORMALIZER = 16.0

V7X_LANES = 128
V7X_SUBLANES = 8
V7X_VMEM_BYTES = 64 * 1024 * 1024


def _vmem_limit(block_bytes, scratch_bytes, temp_bytes):
    need = 2 * block_bytes + scratch_bytes + temp_bytes
    return int(min(need, V7X_VMEM_BYTES))


def _nbytes(shape, dtype):
    return int(np.prod(shape)) * jnp.dtype(dtype).itemsize


def _tile(n, want):
    if n <= want:
        return n
    t = want
    while n % t:
        t -= V7X_SUBLANES
    return t


def _rms_norm(x, g):
    return x * lax.rsqrt(jnp.mean(x * x, axis=-1, keepdims=True) + NORM_EPS) * g


def _ffn_body(x_ref, g_ref, w1_ref, w3_ref, w2_ref, fg_ref, o_ref, n_ref, *, final_norm):
    k = pl.program_id(1)

    @pl.when(k == 0)
    def _():
        x = x_ref[...]
        n_ref[...] = _rms_norm(x, g_ref[...]).astype(n_ref.dtype)
        o_ref[...] = x

    n = n_ref[...]
    a = jnp.dot(n, w1_ref[...], preferred_element_type=F32)
    b = jnp.dot(n, w3_ref[...], preferred_element_type=F32)
    h = (a * jax.nn.sigmoid(a)) * b
    o_ref[...] += FFN_HALF * jnp.dot(h.astype(w2_ref.dtype), w2_ref[...], preferred_element_type=F32)

    if final_norm:
        @pl.when(k == pl.num_programs(1) - 1)
        def _():
            o_ref[...] = _rms_norm(o_ref[...], fg_ref[...])


def _ffn(x, g, w1, w3, w2, final_g, *, final_norm):
    n_tok, d = x.shape
    d_ff = w1.shape[1]
    tm = _tile(n_tok, 512)
    tf = _tile(d_ff, 512)
    blocks = (2 * _nbytes((tm, d), F32) + 2 * _nbytes((d, tf), MXU_DTYPE) + _nbytes((tf, d), MXU_DTYPE)
              + 2 * _nbytes((1, d), F32))
    scratch = _nbytes((tm, d), MXU_DTYPE)
    temps = 3 * _nbytes((tm, tf), F32) + 2 * _nbytes((tm, d), F32)
    return pl.pallas_call(
        functools.partial(_ffn_body, final_norm=final_norm),
        out_shape=jax.ShapeDtypeStruct((n_tok, d), F32),
        grid=(n_tok // tm, d_ff // tf),
        in_specs=[
            pl.BlockSpec((tm, d), lambda i, k: (i, 0)),
            pl.BlockSpec((1, d), lambda i, k: (0, 0)),
            pl.BlockSpec((d, tf), lambda i, k: (0, k)),
            pl.BlockSpec((d, tf), lambda i, k: (0, k)),
            pl.BlockSpec((tf, d), lambda i, k: (k, 0)),
            pl.BlockSpec((1, d), lambda i, k: (0, 0)),
        ],
        out_specs=pl.BlockSpec((tm, d), lambda i, k: (i, 0)),
        scratch_shapes=[pltpu.VMEM((tm, d), MXU_DTYPE)],
        compiler_params=pltpu.CompilerParams(
            dimension_semantics=("parallel", "arbitrary"),
            vmem_limit_bytes=_vmem_limit(blocks, scratch, temps)),
        name="ffn_final" if final_norm else "ffn",
    )(x, g.reshape(1, d), w1, w3, w2, final_g.reshape(1, d))


def _norm_matmul_body(x_ref, g_ref, w_ref, o_ref, n_ref):
    @pl.when(pl.program_id(1) == 0)
    def _():
        n_ref[...] = _rms_norm(x_ref[...], g_ref[...]).astype(n_ref.dtype)

    o_ref[...] = jnp.dot(n_ref[...], w_ref[...], preferred_element_type=F32).astype(o_ref.dtype)


def _norm_matmul(x, g, w, *, col_tile, name):
    n_tok, d = x.shape
    n_out = w.shape[1]
    tm = _tile(n_tok, 512)
    tn = _tile(n_out, col_tile)
    blocks = _nbytes((tm, d), F32) + _nbytes((1, d), F32) + _nbytes((d, tn), MXU_DTYPE) + _nbytes((tm, tn), F32)
    scratch = _nbytes((tm, d), MXU_DTYPE)
    temps = _nbytes((tm, d), F32) + _nbytes((tm, tn), F32)
    return pl.pallas_call(
        _norm_matmul_body,
        out_shape=jax.ShapeDtypeStruct((n_tok, n_out), F32),
        grid=(n_tok // tm, n_out // tn),
        in_specs=[
            pl.BlockSpec((tm, d), lambda i, j: (i, 0)),
            pl.BlockSpec((1, d), lambda i, j: (0, 0)),
            pl.BlockSpec((d, tn), lambda i, j: (0, j)),
        ],
        out_specs=pl.BlockSpec((tm, tn), lambda i, j: (i, j)),
        scratch_shapes=[pltpu.VMEM((tm, d), MXU_DTYPE)],
        compiler_params=pltpu.CompilerParams(
            dimension_semantics=("parallel", "arbitrary"),
            vmem_limit_bytes=_vmem_limit(blocks, scratch, temps)),
        name=name,
    )(x, g.reshape(1, d), w)


def _matmul_residual_body(x_ref, y_ref, w_ref, o_ref):
    o_ref[...] = x_ref[...] + jnp.dot(y_ref[...], w_ref[...], preferred_element_type=F32)


def _matmul_residual(x, y, w, *, name):
    n_tok, d = x.shape
    d_in = y.shape[1]
    tm = _tile(n_tok, 1024)
    tn = _tile(d, 1024)
    blocks = (2 * _nbytes((tm, tn), F32) + _nbytes((tm, d_in), y.dtype) + _nbytes((d_in, tn), MXU_DTYPE))
    temps = _nbytes((tm, tn), F32)
    return pl.pallas_call(
        _matmul_residual_body,
        out_shape=jax.ShapeDtypeStruct((n_tok, d), F32),
        grid=(n_tok // tm, d // tn),
        in_specs=[
            pl.BlockSpec((tm, tn), lambda i, j: (i, j)),
            pl.BlockSpec((tm, d_in), lambda i, j: (i, 0)),
            pl.BlockSpec((d_in, tn), lambda i, j: (0, j)),
        ],
        out_specs=pl.BlockSpec((tm, tn), lambda i, j: (i, j)),
        compiler_params=pltpu.CompilerParams(
            dimension_semantics=("parallel", "arbitrary"),
            vmem_limit_bytes=_vmem_limit(blocks, 0, temps)),
        name=name,
    )(x, y, w)


def _alibi_slopes(n_heads):
    ex = np.arange(1, n_heads + 1, dtype=np.float32) * np.float32(8.0 / n_heads)
    return np.power(np.float32(2.0), -ex)


def _even_core_body(sinks_ref, ub_ref, uc_ref, ux_ref, q_ref, kv_ref, kvp_ref, ucp_ref, uxp_ref, cw_ref,
                    o_ref, hs_ref, kvs_ref, *, slopes):
    tq = ub_ref.shape[1]
    d_conv = ub_ref.shape[2]
    halo = ucp_ref.shape[1]
    blk = ATTN_BLOCK
    first_tile = pl.program_id(1) == 0

    prev = ucp_ref[0] * uxp_ref[0]
    hs_ref[0:halo, :] = jnp.where(first_tile, jnp.zeros_like(prev), prev)
    hs_ref[halo:, :] = uc_ref[0] * ux_ref[0]
    y = None
    for j in range(CONV_WIDTH):
        off = halo - (CONV_WIDTH - 1) + j
        term = cw_ref[j:j + 1, :] * hs_ref[off:off + tq, :]
        y = term if y is None else y + term
    o_ref[0, :, 0:d_conv] = (ub_ref[0] * y).astype(o_ref.dtype)

    half = V7X_LANES // 2
    lane = lax.broadcasted_iota(jnp.int32, (tq + blk, V7X_LANES), 1)
    low = lane < half
    kv_all = jnp.concatenate([kvp_ref[0], kv_ref[0]], axis=0)
    n_slabs = kv_all.shape[1] // V7X_LANES
    for s in range(n_slabs):
        slab = kv_all[:, s * V7X_LANES:(s + 1) * V7X_LANES]
        swapped = pltpu.roll(slab, half, axis=1)
        zero = jnp.zeros_like(slab)
        kvs_ref[2 * s, 0] = jnp.where(low, slab, zero).astype(kvs_ref.dtype)
        kvs_ref[2 * s, 1] = jnp.where(low, zero, swapped).astype(kvs_ref.dtype)
        kvs_ref[2 * s + 1, 0] = jnp.where(low, swapped, zero).astype(kvs_ref.dtype)
        kvs_ref[2 * s + 1, 1] = jnp.where(low, zero, slab).astype(kvs_ref.dtype)

    row = lax.broadcasted_iota(jnp.int32, (blk, 2 * blk), 0)
    col = lax.broadcasted_iota(jnp.int32, (blk, 2 * blk), 1)
    dist_i = row + blk - col
    dist = dist_i.astype(F32)
    valid = (dist_i >= 0) & (dist_i < WINDOW)
    valid_first = valid & ((col >= blk) | jnp.logical_not(first_tile))
    scale = HEAD_DIM ** -0.5
    neg_inf = jnp.full((blk, 2 * blk), -jnp.inf, F32)
    d_attn = q_ref.shape[2]
    group = (d_attn // HEAD_DIM) // N_KV_HEADS

    for qb in range(tq // blk):
        rows = slice(qb * blk, (qb + 1) * blk)
        keys = slice(qb * blk, (qb + 2) * blk)
        ok = valid_first if qb == 0 else valid
        for g in range(N_KV_HEADS):
            slab0 = g * group // 2
            q_slabs = [q_ref[0, rows, (slab0 + a) * V7X_LANES:(slab0 + a + 1) * V7X_LANES]
                       for a in range(group // 2)]
            q4 = jnp.concatenate(q_slabs, axis=0).astype(MXU_DTYPE)
            k_cat = jnp.concatenate([kvs_ref[g, 0, keys, :], kvs_ref[g, 1, keys, :]], axis=0)
            v_cat = jnp.concatenate([kvs_ref[N_KV_HEADS + g, 0, keys, :],
                                     kvs_ref[N_KV_HEADS + g, 1, keys, :]], axis=0)
            s = lax.dot_general(q4, k_cat, (((1,), (1,)), ((), ())), preferred_element_type=F32)
            p_rows = []
            for a in range(group // 2):
                p_cols = []
                for hi in range(2):
                    head = g * group + 2 * a + hi
                    sq = s[a * blk:(a + 1) * blk, hi * 2 * blk:(hi + 1) * 2 * blk]
                    sq = jnp.where(ok, sq * scale - float(slopes[head]) * dist, neg_inf)
                    sink = sinks_ref[head]
                    m = jnp.maximum(jnp.max(sq, axis=-1, keepdims=True), sink)
                    e = jnp.exp(sq - m)
                    denom = jnp.sum(e, axis=-1, keepdims=True) + jnp.exp(sink - m)
                    p_cols.append((e * (1.0 / denom)).astype(MXU_DTYPE))
                p_rows.append(jnp.concatenate(p_cols, axis=1))
            p = jnp.concatenate(p_rows, axis=0)
            out = jnp.dot(p, v_cat, preferred_element_type=F32)
            for a in range(group // 2):
                c0 = d_conv + (slab0 + a) * V7X_LANES
                o_ref[0, rows, c0:c0 + V7X_LANES] = out[a * blk:(a + 1) * blk].astype(o_ref.dtype)


def _even_core(u, conv_w, sinks):
    bsz, seq, _ = u.shape
    d_conv = conv_w.shape[1]
    n_q_heads = sinks.shape[0]
    d_attn = n_q_heads * HEAD_DIM
    d_kv = N_KV_HEADS * HEAD_DIM
    assert d_conv == d_attn and (n_q_heads // N_KV_HEADS) % 2 == 0 and (2 * d_kv) % V7X_LANES == 0
    tq = _tile(seq, 512)
    assert tq % ATTN_BLOCK == 0
    halo = V7X_SUBLANES
    blocks_per_tile = tq // ATTN_BLOCK
    halos_per_tile = tq // halo
    kv_col = (3 * d_conv + d_attn) // (2 * d_kv)
    slopes = _alibi_slopes(n_q_heads)

    def prev_block(i, per_tile):
        return jnp.maximum(i * per_tile - 1, 0)

    blocks = (4 * _nbytes((tq, d_conv), F32) + _nbytes((tq + ATTN_BLOCK, 2 * d_kv), F32)
              + 2 * _nbytes((halo, d_conv), F32) + _nbytes((tq, d_conv + d_attn), MXU_DTYPE))
    scratch = _nbytes((tq + halo, d_conv), F32) + 4 * _nbytes((tq + ATTN_BLOCK, 2 * d_kv), MXU_DTYPE)
    temps = 3 * _nbytes((tq, d_conv), F32) + 2 * _nbytes((tq + ATTN_BLOCK, 2 * d_kv), F32)
    return pl.pallas_call(
        functools.partial(_even_core_body, slopes=slopes),
        out_shape=jax.ShapeDtypeStruct((bsz, seq, d_conv + d_attn), MXU_DTYPE),
        grid=(bsz, seq // tq),
        in_specs=[
            pl.BlockSpec(memory_space=pltpu.SMEM),
            pl.BlockSpec((1, tq, d_conv), lambda b, i: (b, i, 0)),
            pl.BlockSpec((1, tq, d_conv), lambda b, i: (b, i, 1)),
            pl.BlockSpec((1, tq, d_conv), lambda b, i: (b, i, 2)),
            pl.BlockSpec((1, tq, d_attn), lambda b, i: (b, i, 3)),
            pl.BlockSpec((1, tq, 2 * d_kv), lambda b, i: (b, i, kv_col)),
            pl.BlockSpec((1, ATTN_BLOCK, 2 * d_kv), lambda b, i: (b, prev_block(i, blocks_per_tile), kv_col)),
            pl.BlockSpec((1, halo, d_conv), lambda b, i: (b, prev_block(i, halos_per_tile), 1)),
            pl.BlockSpec((1, halo, d_conv), lambda b, i: (b, prev_block(i, halos_per_tile), 2)),
            pl.BlockSpec((CONV_WIDTH, d_conv), lambda b, i: (0, 0)),
        ],
        out_specs=pl.BlockSpec((1, tq, d_conv + d_attn), lambda b, i: (b, i, 0)),
        scratch_shapes=[
            pltpu.VMEM((tq + halo, d_conv), F32),
            pltpu.VMEM((2 * N_KV_HEADS, 2, tq + ATTN_BLOCK, V7X_LANES), MXU_DTYPE),
        ],
        compiler_params=pltpu.CompilerParams(
            dimension_semantics=("parallel", "parallel"),
            vmem_limit_bytes=_vmem_limit(blocks, scratch, temps)),
        name="even_core",
    )(sinks, u, u, u, u, u, u, u, u, conv_w)


def _gla_body(q_ref, k_ref, v_ref, r_ref, gl_ref, wgu_ref, bgu_ref, hg_ref, o_ref, s_ref):
    tt = q_ref.shape[1]
    hk = q_ref.shape[2]
    ch = GLA_CHUNK
    n_chunks = tt // ch

    @pl.when(pl.program_id(2) == 0)
    def _():
        s_ref[...] = jnp.zeros_like(s_ref)

    z = jnp.dot(gl_ref[0].astype(MXU_DTYPE), wgu_ref[...], preferred_element_type=F32) + bgu_ref[...]
    log_a = (jnp.minimum(z, 0.0) - jnp.log1p(jnp.exp(-jnp.abs(z)))) * (1.0 / GLA_GATE_NORMALIZER)

    row_in_chunk = lax.broadcasted_iota(jnp.int32, (tt, hk), 0) % ch
    b = log_a
    step = 1
    while step < ch:
        b = b + jnp.where(row_in_chunk >= step, pltpu.roll(b, step, axis=0), 0.0)
        step *= 2
    b_last = jnp.concatenate(
        [jnp.broadcast_to(b[(c + 1) * ch - 1:(c + 1) * ch, :], (ch, hk)) for c in range(n_chunks)], axis=0)

    q = q_ref[0]
    k = k_ref[0]
    v = v_ref[0].astype(MXU_DTYPE)
    q_dec = ((q * (hk ** -0.5)) * jnp.exp(b)).astype(MXU_DTYPE)
    k_inv = (k * jnp.exp(-b)).astype(MXU_DTYPE)
    k_tail_t = (k * jnp.exp(b_last - b)).T.astype(MXU_DTYPE)
    decay_t = jnp.exp(b_last.T)

    attn = lax.dot_general(q_dec, k_inv, (((1,), (1,)), ((), ())), preferred_element_type=F32)
    ri = lax.broadcasted_iota(jnp.int32, (tt, tt), 0)
    ci = lax.broadcasted_iota(jnp.int32, (tt, tt), 1)
    keep = (ri // ch == ci // ch) & (ci <= ri)
    attn = jnp.where(keep, attn, 0.0).astype(MXU_DTYPE)
    sub = min(tt, 2 * V7X_LANES)
    o_intra = jnp.concatenate(
        [jnp.dot(attn[a:a + sub, a:a + sub], v[a:a + sub], preferred_element_type=F32)
         for a in range(0, tt, sub)], axis=0)

    pair = 2 * ch
    row_in_pair = lax.broadcasted_iota(jnp.int32, (pair, v.shape[1]), 0)
    o_inter = []
    for c in range(n_chunks):
        state = s_ref[...]
        o_inter.append(jnp.dot(q_dec[c * ch:(c + 1) * ch], state.astype(MXU_DTYPE),
                               preferred_element_type=F32))
        p0 = (c * ch // pair) * pair
        in_chunk = (row_in_pair >= c * ch - p0) & (row_in_pair < (c + 1) * ch - p0)
        v_c = jnp.where(in_chunk, v[p0:p0 + pair], jnp.zeros_like(v[p0:p0 + pair]))
        kv = jnp.dot(k_tail_t[:, p0:p0 + pair], v_c, preferred_element_type=F32)
        s_ref[...] = decay_t[:, c * ch:c * ch + 1] * state + kv
    o = o_intra + jnp.concatenate(o_inter, axis=0)

    o = _rms_norm(o, hg_ref[...])
    r = r_ref[0]
    o_ref[0] = (o * (r * jax.nn.sigmoid(r))).astype(o_ref.dtype)


def _gla_core(u, g_low, w_gate_up, b_gate_up, head_g):
    bsz, seq, _ = u.shape
    d_k = w_gate_up.shape[1]
    hk = d_k // GLA_HEADS
    hv = head_g.shape[0]
    d_v = hv * GLA_HEADS
    assert hv % hk == 0 and d_k % hv == 0
    tt = _tile(seq, 512)
    assert tt % (2 * GLA_CHUNK) == 0
    gl_w = g_low.shape[2]
    k_col = d_k // hk
    v_col = 2 * d_k // hv
    r_col = (2 * d_k + d_v) // hv
    blocks = (2 * _nbytes((tt, hk), F32) + 2 * _nbytes((tt, hv), F32) + _nbytes((tt, gl_w), F32)
              + _nbytes((gl_w, hk), MXU_DTYPE) + _nbytes((1, hk), F32) + _nbytes((1, hv), F32)
              + _nbytes((tt, hv), MXU_DTYPE))
    scratch = _nbytes((hk, hv), F32)
    temps = 8 * _nbytes((tt, hk), F32) + 2 * _nbytes((tt, tt), F32) + 4 * _nbytes((tt, hv), F32)
    return pl.pallas_call(
        _gla_body,
        out_shape=jax.ShapeDtypeStruct((bsz, seq, d_v), MXU_DTYPE),
        grid=(bsz, GLA_HEADS, seq // tt),
        in_specs=[
            pl.BlockSpec((1, tt, hk), lambda b, h, t: (b, t, h)),
            pl.BlockSpec((1, tt, hk), lambda b, h, t: (b, t, k_col + h)),
            pl.BlockSpec((1, tt, hv), lambda b, h, t: (b, t, v_col + h)),
            pl.BlockSpec((1, tt, hv), lambda b, h, t: (b, t, r_col + h)),
            pl.BlockSpec((1, tt, gl_w), lambda b, h, t: (b, t, 0)),
            pl.BlockSpec((gl_w, hk), lambda b, h, t: (0, h)),
            pl.BlockSpec((1, hk), lambda b, h, t: (0, h)),
            pl.BlockSpec((1, hv), lambda b, h, t: (0, 0)),
        ],
        out_specs=pl.BlockSpec((1, tt, hv), lambda b, h, t: (b, t, h)),
        scratch_shapes=[pltpu.VMEM((hk, hv), F32)],
        compiler_params=pltpu.CompilerParams(
            dimension_semantics=("parallel", "parallel", "arbitrary"),
            vmem_limit_bytes=_vmem_limit(blocks, scratch, temps)),
        name="gla_core",
    )(u, u, u, u, g_low, w_gate_up, b_gate_up.reshape(1, d_k), head_g.reshape(1, hv))


def _even_mixer(x, g, w_in, conv_w, sinks, w_out, bsz, seq):
    u = _norm_matmul(x, g, w_in.astype(MXU_DTYPE), col_tile=1536, name="even_in_proj")
    y = _even_core(u.reshape(bsz, seq, -1), conv_w, sinks)
    return _matmul_residual(x, y.reshape(bsz * seq, -1), w_out.astype(MXU_DTYPE), name="even_out_proj")


def _odd_mixer(x, g, w_in, w_gate_up, b_gate_up, head_g, w_out, bsz, seq):
    rank, d_k = w_gate_up.shape
    d_main = w_in.shape[1] - rank
    w_main = w_in[:, :d_main].astype(MXU_DTYPE)
    w_low = jnp.pad(w_in[:, d_main:], ((0, 0), (0, V7X_LANES - rank))).astype(MXU_DTYPE)
    w_up = jnp.pad(w_gate_up, ((0, V7X_LANES - rank), (0, 0))).astype(MXU_DTYPE)
    u = _norm_matmul(x, g, w_main, col_tile=1536, name="odd_in_proj")
    g_low = _norm_matmul(x, g, w_low, col_tile=V7X_LANES, name="odd_gate_proj")
    o = _gla_core(u.reshape(bsz, seq, -1), g_low.reshape(bsz, seq, -1), w_up, b_gate_up, head_g)
    return _matmul_residual(x, o.reshape(bsz * seq, -1), w_out.astype(MXU_DTYPE), name="odd_out_proj")


def kernel(x, norm_g, ffn_pre_w1, ffn_pre_w3, ffn_pre_w2, ffn_post_w1, ffn_post_w3, ffn_post_w2, even_w_in,
           even_conv_w, even_sinks, even_w_out, odd_w_in, odd_w_gate_up, odd_b_gate_up, odd_head_g, odd_w_out,
           final_g):
    bsz, seq, d = x.shape
    depth = norm_g.shape[0]
    h = x.reshape(bsz * seq, d)
    cast = lambda w: w.astype(MXU_DTYPE)
    for layer in range(depth):
        h = _ffn(h, norm_g[layer, 0], cast(ffn_pre_w1[layer]), cast(ffn_pre_w3[layer]), cast(ffn_pre_w2[layer]),
                 final_g, final_norm=False)
        i = layer // 2
        if layer % 2 == 0:
            h = _even_mixer(h, norm_g[layer, 1], even_w_in[i], even_conv_w[i], even_sinks[i], even_w_out[i],
                            bsz, seq)
        else:
            h = _odd_mixer(h, norm_g[layer, 1], odd_w_in[i], odd_w_gate_up[i], odd_b_gate_up[i], odd_head_g[i],
                           odd_w_out[i], bsz, seq)
        h = _ffn(h, norm_g[layer, 2], cast(ffn_post_w1[layer]), cast(ffn_post_w3[layer]),
                 cast(ffn_post_w2[layer]), final_g, final_norm=(layer == depth - 1))
    return h.reshape(bsz, seq, d)
```

```python
import functools

import jax
import jax.numpy as jnp
import numpy as np
from jax import lax
from jax.experimental import pallas as pl
from jax.experimental.pallas import tpu as pltpu

F32 = jnp.float32
MXU_DTYPE = jnp.bfloat16

NORM_EPS = 1e-6
FFN_HALF = 0.5
CONV_WIDTH = 3
HEAD_DIM = 64
N_KV_HEADS = 4
WINDOW = 128
ATTN_BLOCK = 128
GLA_HEADS = 4
GLA_CHUNK = 64
GLA_GATE_NORMALIZER = 16.0

V7X_LANES = 128
V7X_SUBLANES = 8
V7X_MXU_DIM = 256
V7X_VMEM_BYTES = 64 * 1024 * 1024


def _vmem_limit(block_bytes, scratch_bytes, temp_bytes):
    need = 2 * block_bytes + scratch_bytes + temp_bytes
    return int(min(need, V7X_VMEM_BYTES))


def _nbytes(shape, dtype):
    return int(np.prod(shape)) * jnp.dtype(dtype).itemsize


def _tile(n, want):
    if n <= want:
        return n
    t = want
    while n % t:
        t -= V7X_SUBLANES
    return t


def _rms_norm(x, g):
    return x * lax.rsqrt(jnp.mean(x * x, axis=-1, keepdims=True) + NORM_EPS) * g


def _dot_nt(a, b):
    return lax.dot_general(a, b, (((1,), (1,)), ((), ())), preferred_element_type=F32)


def _ffn_body(x_ref, g_ref, w1_ref, w3_ref, w2_ref, fg_ref, o_ref, n_ref, *, final_norm):
    k = pl.program_id(1)

    @pl.when(k == 0)
    def _():
        x = x_ref[...]
        n_ref[...] = _rms_norm(x, g_ref[...]).astype(n_ref.dtype)
        o_ref[...] = x

    n = n_ref[...]
    a = jnp.dot(n, w1_ref[...], preferred_element_type=F32)
    b = jnp.dot(n, w3_ref[...], preferred_element_type=F32)
    h = (a * jax.nn.sigmoid(a)) * b
    o_ref[...] += FFN_HALF * jnp.dot(h.astype(w2_ref.dtype), w2_ref[...], preferred_element_type=F32)

    if final_norm:
        @pl.when(k == pl.num_programs(1) - 1)
        def _():
            o_ref[...] = _rms_norm(o_ref[...], fg_ref[...])


def _ffn(x, g, w1, w3, w2, layer, final_g, *, final_norm):
    n_tok, d = x.shape
    d_ff = w1.shape[2]
    tm = _tile(n_tok, 1024)
    tf = _tile(d_ff, 512)
    blocks = (2 * _nbytes((tm, d), F32) + 2 * _nbytes((d, tf), MXU_DTYPE) + _nbytes((tf, d), MXU_DTYPE)
              + 2 * _nbytes((1, d), F32))
    scratch = _nbytes((tm, d), MXU_DTYPE)
    temps = 6 * _nbytes((tm, tf), F32)
    return pl.pallas_call(
        functools.partial(_ffn_body, final_norm=final_norm),
        out_shape=jax.ShapeDtypeStruct((n_tok, d), F32),
        grid=(n_tok // tm, d_ff // tf),
        in_specs=[
            pl.BlockSpec((tm, d), lambda i, k: (i, 0)),
            pl.BlockSpec((1, d), lambda i, k: (0, 0)),
            pl.BlockSpec((None, d, tf), lambda i, k: (layer, 0, k)),
            pl.BlockSpec((None, d, tf), lambda i, k: (layer, 0, k)),
            pl.BlockSpec((None, tf, d), lambda i, k: (layer, k, 0)),
            pl.BlockSpec((1, d), lambda i, k: (0, 0)),
        ],
        out_specs=pl.BlockSpec((tm, d), lambda i, k: (i, 0)),
        scratch_shapes=[pltpu.VMEM((tm, d), MXU_DTYPE)],
        compiler_params=pltpu.CompilerParams(
            dimension_semantics=("parallel", "arbitrary"),
            vmem_limit_bytes=_vmem_limit(blocks, scratch, temps)),
        name="ffn_final" if final_norm else "ffn",
    )(x, g.reshape(1, d), w1, w3, w2, final_g.reshape(1, d))


def _norm_matmul_body(x_ref, g_ref, w_ref, o_ref, n_ref):
    @pl.when(pl.program_id(1) == 0)
    def _():
        n_ref[...] = _rms_norm(x_ref[...], g_ref[...]).astype(n_ref.dtype)

    o_ref[...] = jnp.dot(n_ref[...], w_ref[...], preferred_element_type=F32)


def _norm_matmul_tail_body(x_ref, g_ref, w_ref, wt_ref, o_ref, ot_ref, n_ref):
    @pl.when(pl.program_id(1) == 0)
    def _():
        n_ref[...] = _rms_norm(x_ref[...], g_ref[...]).astype(n_ref.dtype)
        ot_ref[...] = jnp.dot(n_ref[...], wt_ref[...], preferred_element_type=F32)

    o_ref[...] = jnp.dot(n_ref[...], w_ref[...], preferred_element_type=F32)


def _norm_matmul(x, g, w, layer, n_out, *, w_tail=None, name):
    n_tok, d = x.shape
    tm = _tile(n_tok, 1024)
    tn = _tile(n_out, 1536)
    blocks = _nbytes((tm, d), F32) + _nbytes((1, d), F32) + _nbytes((d, tn), MXU_DTYPE) + _nbytes((tm, tn), F32)
    scratch = _nbytes((tm, d), MXU_DTYPE)
    temps = _nbytes((tm, tn), F32)
    in_specs = [
        pl.BlockSpec((tm, d), lambda i, j: (i, 0)),
        pl.BlockSpec((1, d), lambda i, j: (0, 0)),
        pl.BlockSpec((None, d, tn), lambda i, j: (layer, 0, j)),
    ]
    out_shape = jax.ShapeDtypeStruct((n_tok, n_out), F32)
    out_specs = pl.BlockSpec((tm, tn), lambda i, j: (i, j))
    args = (x, g.reshape(1, d), w)
    body = _norm_matmul_body
    if w_tail is not None:
        n_tail = w_tail.shape[1]
        blocks += _nbytes((d, n_tail), MXU_DTYPE) + _nbytes((tm, n_tail), F32)
        in_specs.append(pl.BlockSpec((d, n_tail), lambda i, j: (0, 0)))
        out_shape = (out_shape, jax.ShapeDtypeStruct((n_tok, n_tail), F32))
        out_specs = (out_specs, pl.BlockSpec((tm, n_tail), lambda i, j: (i, 0)))
        args = args + (w_tail,)
        body = _norm_matmul_tail_body
    return pl.pallas_call(
        body,
        out_shape=out_shape,
        grid=(n_tok // tm, n_out // tn),
        in_specs=in_specs,
        out_specs=out_specs,
        scratch_shapes=[pltpu.VMEM((tm, d), MXU_DTYPE)],
        compiler_params=pltpu.CompilerParams(
            dimension_semantics=("parallel", "arbitrary"),
            vmem_limit_bytes=_vmem_limit(blocks, scratch, temps)),
        name=name,
    )(*args)


def _matmul_residual_body(x_ref, y_ref, w_ref, o_ref):
    o_ref[...] = x_ref[...] + jnp.dot(y_ref[...], w_ref[...], preferred_element_type=F32)


def _matmul_residual(x, y, w, layer, *, name):
    n_tok, d = x.shape
    d_in = y.shape[1]
    tm = _tile(n_tok, 1024)
    tn = _tile(d, 1024)
    blocks = (2 * _nbytes((tm, tn), F32) + _nbytes((tm, d_in), y.dtype) + _nbytes((d_in, tn), MXU_DTYPE))
    temps = _nbytes((tm, tn), F32)
    return pl.pallas_call(
        _matmul_residual_body,
        out_shape=jax.ShapeDtypeStruct((n_tok, d), F32),
        grid=(n_tok // tm, d // tn),
        in_specs=[
            pl.BlockSpec((tm, tn), lambda i, j: (i, j)),
            pl.BlockSpec((tm, d_in), lambda i, j: (i, 0)),
            pl.BlockSpec((None, d_in, tn), lambda i, j: (layer, 0, j)),
        ],
        out_specs=pl.BlockSpec((tm, tn), lambda i, j: (i, j)),
        compiler_params=pltpu.CompilerParams(
            dimension_semantics=("parallel", "arbitrary"),
            vmem_limit_bytes=_vmem_limit(blocks, 0, temps)),
        name=name,
    )(x, y, w)


def _alibi_slopes(n_heads):
    ex = np.arange(1, n_heads + 1, dtype=np.float32) * np.float32(8.0 / n_heads)
    return np.power(np.float32(2.0), -ex)


def _even_core_body(sinks_ref, ub_ref, uc_ref, ux_ref, q_ref, kv_ref, kvp_ref, ucp_ref, uxp_ref, cw_ref,
                    o_ref, hs_ref, kvs_ref, bias_ref, *, slopes):
    tq = ub_ref.shape[1]
    d_conv = ub_ref.shape[2]
    halo = ucp_ref.shape[1]
    blk = ATTN_BLOCK
    first_tile = pl.program_id(1) == 0
    n_heads = bias_ref.shape[0]

    row = lax.broadcasted_iota(jnp.int32, (blk, 2 * blk), 0)
    col = lax.broadcasted_iota(jnp.int32, (blk, 2 * blk), 1)

    @pl.when((pl.program_id(0) == 0) & first_tile)
    def _():
        dist_i = row + blk - col
        dist = dist_i.astype(F32)
        valid = (dist_i >= 0) & (dist_i < WINDOW)
        for head in range(n_heads):
            bias_ref[head] = jnp.where(valid, -(float(slopes[head]) * dist), -jnp.inf)

    prev = ucp_ref[0] * uxp_ref[0]
    hs_ref[0:halo, :] = jnp.where(first_tile, jnp.zeros_like(prev), prev)
    hs_ref[halo:, :] = uc_ref[0] * ux_ref[0]
    y = None
    for j in range(CONV_WIDTH):
        off = halo - (CONV_WIDTH - 1) + j
        term = cw_ref[j:j + 1, :] * hs_ref[off:off + tq, :]
        y = term if y is None else y + term
    o_ref[0, :, 0:d_conv] = (ub_ref[0] * y).astype(o_ref.dtype)

    half = V7X_LANES // 2
    lane = lax.broadcasted_iota(jnp.int32, (tq + blk, V7X_LANES), 1)
    low = lane < half
    kv_all = jnp.concatenate([kvp_ref[0], kv_ref[0]], axis=0)
    for s in range(kv_all.shape[1] // V7X_LANES):
        slab = kv_all[:, s * V7X_LANES:(s + 1) * V7X_LANES]
        swapped = pltpu.roll(slab, half, axis=1)
        zero = jnp.zeros_like(slab)
        kvs_ref[2 * s, 0] = jnp.where(low, slab, zero).astype(kvs_ref.dtype)
        kvs_ref[2 * s, 1] = jnp.where(low, zero, swapped).astype(kvs_ref.dtype)
        kvs_ref[2 * s + 1, 0] = jnp.where(low, swapped, zero).astype(kvs_ref.dtype)
        kvs_ref[2 * s + 1, 1] = jnp.where(low, zero, slab).astype(kvs_ref.dtype)

    no_history = first_tile & (col < blk)
    scale = HEAD_DIM ** -0.5
    group = n_heads // N_KV_HEADS
    for qb in range(tq // blk):
        rows = slice(qb * blk, (qb + 1) * blk)
        keys = slice(qb * blk, (qb + 2) * blk)
        for g in range(N_KV_HEADS):
            slab0 = g * group // 2
            q_slabs = [q_ref[0, rows, (slab0 + a) * V7X_LANES:(slab0 + a + 1) * V7X_LANES]
                       for a in range(group // 2)]
            q4 = (jnp.concatenate(q_slabs, axis=0) * scale).astype(MXU_DTYPE)
            k_cat = jnp.concatenate([kvs_ref[g, 0, keys, :], kvs_ref[g, 1, keys, :]], axis=0)
            v_cat = jnp.concatenate([kvs_ref[N_KV_HEADS + g, 0, keys, :],
                                     kvs_ref[N_KV_HEADS + g, 1, keys, :]], axis=0)
            s = _dot_nt(q4, k_cat)
            p_rows = []
            for a in range(group // 2):
                p_cols = []
                for hi in range(2):
                    head = g * group + 2 * a + hi
                    sq = s[a * blk:(a + 1) * blk, hi * 2 * blk:(hi + 1) * 2 * blk] + bias_ref[head]
                    if qb == 0:
                        sq = jnp.where(no_history, -jnp.inf, sq)
                    sink = sinks_ref[head]
                    m = jnp.maximum(jnp.max(sq, axis=-1, keepdims=True), sink)
                    e = jnp.exp(sq - m)
                    denom = jnp.sum(e, axis=-1, keepdims=True) + jnp.exp(sink - m)
                    p_cols.append((e * (1.0 / denom)).astype(MXU_DTYPE))
                p_rows.append(jnp.concatenate(p_cols, axis=1))
            p = jnp.concatenate(p_rows, axis=0)
            out = jnp.dot(p, v_cat, preferred_element_type=F32)
            for a in range(group // 2):
                c0 = d_conv + (slab0 + a) * V7X_LANES
                o_ref[0, rows, c0:c0 + V7X_LANES] = out[a * blk:(a + 1) * blk].astype(o_ref.dtype)


def _even_core(u, conv_w, sinks):
    bsz, seq, _ = u.shape
    d_conv = conv_w.shape[1]
    n_q_heads = sinks.shape[0]
    d_attn = n_q_heads * HEAD_DIM
    d_kv = N_KV_HEADS * HEAD_DIM
    assert d_conv == d_attn and (n_q_heads // N_KV_HEADS) % 2 == 0 and (2 * d_kv) % V7X_LANES == 0
    tq = _tile(seq, 512)
    assert tq % ATTN_BLOCK == 0
    halo = V7X_SUBLANES
    blocks_per_tile = tq // ATTN_BLOCK
    halos_per_tile = tq // halo
    kv_col = (3 * d_conv + d_attn) // (2 * d_kv)
    slopes = _alibi_slopes(n_q_heads)

    def prev_block(i, per_tile):
        return jnp.maximum(i * per_tile - 1, 0)

    blocks = (4 * _nbytes((tq, d_conv), F32) + _nbytes((tq + ATTN_BLOCK, 2 * d_kv), F32)
              + 2 * _nbytes((halo, d_conv), F32) + _nbytes((tq, d_conv + d_attn), MXU_DTYPE))
    scratch = (_nbytes((tq + halo, d_conv), F32) + 4 * _nbytes((tq + ATTN_BLOCK, 2 * d_kv), MXU_DTYPE)
               + _nbytes((n_q_heads, ATTN_BLOCK, 2 * ATTN_BLOCK), F32))
    temps = 3 * _nbytes((tq, d_conv), F32) + 2 * _nbytes((tq + ATTN_BLOCK, 2 * d_kv), F32)
    return pl.pallas_call(
        functools.partial(_even_core_body, slopes=slopes),
        out_shape=jax.ShapeDtypeStruct((bsz, seq, d_conv + d_attn), MXU_DTYPE),
        grid=(bsz, seq // tq),
        in_specs=[
            pl.BlockSpec(memory_space=pltpu.SMEM),
            pl.BlockSpec((1, tq, d_conv), lambda b, i: (b, i, 0)),
            pl.BlockSpec((1, tq, d_conv), lambda b, i: (b, i, 1)),
            pl.BlockSpec((1, tq, d_conv), lambda b, i: (b, i, 2)),
            pl.BlockSpec((1, tq, d_attn), lambda b, i: (b, i, 3)),
            pl.BlockSpec((1, tq, 2 * d_kv), lambda b, i: (b, i, kv_col)),
            pl.BlockSpec((1, ATTN_BLOCK, 2 * d_kv), lambda b, i: (b, prev_block(i, blocks_per_tile), kv_col)),
            pl.BlockSpec((1, halo, d_conv), lambda b, i: (b, prev_block(i, halos_per_tile), 1)),
            pl.BlockSpec((1, halo, d_conv), lambda b, i: (b, prev_block(i, halos_per_tile), 2)),
            pl.BlockSpec((CONV_WIDTH, d_conv), lambda b, i: (0, 0)),
        ],
        out_specs=pl.BlockSpec((1, tq, d_conv + d_attn), lambda b, i: (b, i, 0)),
        scratch_shapes=[
            pltpu.VMEM((tq + halo, d_conv), F32),
            pltpu.VMEM((2 * N_KV_HEADS, 2, tq + ATTN_BLOCK, V7X_LANES), MXU_DTYPE),
            pltpu.VMEM((n_q_heads, ATTN_BLOCK, 2 * ATTN_BLOCK), F32),
        ],
        compiler_params=pltpu.CompilerParams(
            dimension_semantics=("arbitrary", "arbitrary"),
            vmem_limit_bytes=_vmem_limit(blocks, scratch, temps)),
        name="even_core",
    )(sinks, u, u, u, u, u, u, u, u, conv_w)


def _split3(x):
    hi = x.astype(MXU_DTYPE)
    r1 = x - hi.astype(F32)
    mid = r1.astype(MXU_DTYPE)
    lo = (r1 - mid.astype(F32)).astype(MXU_DTYPE)
    return hi, mid, lo


def _gla_body(q_ref, k_ref, v_ref, r_ref, gl_ref, wgu_ref, bgu_ref, hg_ref, o_ref, s_ref):
    tt = q_ref.shape[1]
    hk = q_ref.shape[2]
    ch = GLA_CHUNK
    n_chunks = tt // ch
    sub = min(tt, V7X_MXU_DIM)

    @pl.when(pl.program_id(2) == 0)
    def _():
        s_ref[...] = jnp.zeros_like(s_ref)

    ri = lax.broadcasted_iota(jnp.int32, (sub, sub), 0)
    ci = lax.broadcasted_iota(jnp.int32, (sub, sub), 1)
    keep = (ri // ch == ci // ch) & (ci <= ri)
    tri = jnp.where(keep, 1.0, 0.0).astype(MXU_DTYPE)

    z = jnp.dot(gl_ref[0].astype(MXU_DTYPE), wgu_ref[...], preferred_element_type=F32) + bgu_ref[...]
    log_a = (jnp.minimum(z, 0.0) - jnp.log1p(jnp.exp(-jnp.abs(z)))) * (1.0 / GLA_GATE_NORMALIZER)

    parts = jnp.concatenate(_split3(log_a), axis=1)
    b = []
    for a in range(0, tt, sub):
        c3 = jnp.dot(tri, parts[a:a + sub], preferred_element_type=F32)
        b.append(c3[:, 0:hk] + c3[:, hk:2 * hk] + c3[:, 2 * hk:3 * hk])
    b = jnp.concatenate(b, axis=0)
    b_last = jnp.concatenate(
        [jnp.broadcast_to(b[(c + 1) * ch - 1:(c + 1) * ch, :], (ch, hk)) for c in range(n_chunks)], axis=0)

    q = q_ref[0]
    k = k_ref[0]
    v = v_ref[0].astype(MXU_DTYPE)
    q_dec = ((q * (hk ** -0.5)) * jnp.exp(b)).astype(MXU_DTYPE)
    k_inv = (k * jnp.exp(-b)).astype(MXU_DTYPE)
    k_tail_t = (k * jnp.exp(b_last - b)).T.astype(MXU_DTYPE)
    decay_t = jnp.exp(b_last.T)

    o_intra = []
    for a in range(0, tt, sub):
        attn = _dot_nt(q_dec[a:a + sub], k_inv[a:a + sub])
        attn = jnp.where(keep, attn, 0.0).astype(MXU_DTYPE)
        o_intra.append(jnp.dot(attn, v[a:a + sub], preferred_element_type=F32))
    o_intra = jnp.concatenate(o_intra, axis=0)

    pair = 2 * ch
    row_in_pair = lax.broadcasted_iota(jnp.int32, (pair, v.shape[1]), 0)
    o_inter = []
    for c in range(n_chunks):
        state = s_ref[...]
        o_inter.append(jnp.dot(q_dec[c * ch:(c + 1) * ch], state.astype(MXU_DTYPE),
                               preferred_element_type=F32))
        p0 = (c * ch // pair) * pair
        in_chunk = (row_in_pair >= c * ch - p0) & (row_in_pair < (c + 1) * ch - p0)
        v_c = jnp.where(in_chunk, v[p0:p0 + pair], jnp.zeros_like(v[p0:p0 + pair]))
        kv = jnp.dot(k_tail_t[:, p0:p0 + pair], v_c, preferred_element_type=F32)
        s_ref[...] = decay_t[:, c * ch:c * ch + 1] * state + kv
    o = o_intra + jnp.concatenate(o_inter, axis=0)

    o = _rms_norm(o, hg_ref[...])
    r = r_ref[0]
    o_ref[0] = (o * (r * jax.nn.sigmoid(r))).astype(o_ref.dtype)


def _gla_core(u, g_low, w_gate_up, b_gate_up, head_g):
    bsz, seq, _ = u.shape
    d_k = w_gate_up.shape[1]
    hk = d_k // GLA_HEADS
    hv = head_g.shape[0]
    d_v = hv * GLA_HEADS
    assert hv % hk == 0 and d_k % hv == 0
    tt = _tile(seq, 512)
    assert tt % (2 * GLA_CHUNK) == 0 and min(tt, V7X_MXU_DIM) % GLA_CHUNK == 0
    gl_w = g_low.shape[2]
    k_col = d_k // hk
    v_col = 2 * d_k // hv
    r_col = (2 * d_k + d_v) // hv
    blocks = (2 * _nbytes((tt, hk), F32) + 2 * _nbytes((tt, hv), F32) + _nbytes((tt, gl_w), F32)
              + _nbytes((gl_w, hk), MXU_DTYPE) + _nbytes((1, hk), F32) + _nbytes((1, hv), F32)
              + _nbytes((tt, hv), MXU_DTYPE))
    scratch = _nbytes((hk, hv), F32)
    temps = 8 * _nbytes((tt, hk), F32) + 2 * _nbytes((tt, tt), F32) + 4 * _nbytes((tt, hv), F32)
    return pl.pallas_call(
        _gla_body,
        out_shape=jax.ShapeDtypeStruct((bsz, seq, d_v), MXU_DTYPE),
        grid=(bsz, GLA_HEADS, seq // tt),
        in_specs=[
            pl.BlockSpec((1, tt, hk), lambda b, h, t: (b, t, h)),
            pl.BlockSpec((1, tt, hk), lambda b, h, t: (b, t, k_col + h)),
            pl.BlockSpec((1, tt, hv), lambda b, h, t: (b, t, v_col + h)),
            pl.BlockSpec((1, tt, hv), lambda b, h, t: (b, t, r_col + h)),
            pl.BlockSpec((1, tt, gl_w), lambda b, h, t: (b, t, 0)),
            pl.BlockSpec((gl_w, hk), lambda b, h, t: (0, h)),
            pl.BlockSpec((1, hk), lambda b, h, t: (0, h)),
            pl.BlockSpec((1, hv), lambda b, h, t: (0, 0)),
        ],
        out_specs=pl.BlockSpec((1, tt, hv), lambda b, h, t: (b, t, h)),
        scratch_shapes=[pltpu.VMEM((hk, hv), F32)],
        compiler_params=pltpu.CompilerParams(
            dimension_semantics=("parallel", "parallel", "arbitrary"),
            vmem_limit_bytes=_vmem_limit(blocks, scratch, temps)),
        name="gla_core",
    )(u, u, u, u, g_low, w_gate_up, b_gate_up.reshape(1, d_k), head_g.reshape(1, hv))


def _even_mixer(x, g, w_in, conv_w, sinks, w_out, i, bsz, seq):
    u = _norm_matmul(x, g, w_in, i, w_in.shape[2], name="even_in_proj")
    y = _even_core(u.reshape(bsz, seq, -1), conv_w, sinks)
    return _matmul_residual(x, y.reshape(bsz * seq, -1), w_out, i, name="even_out_proj")


def _odd_mixer(x, g, w_in, w_in_f32, w_gate_up, b_gate_up, head_g, w_out, i, bsz, seq):
    rank, d_k = w_gate_up.shape
    d_main = w_in.shape[2] - rank
    w_low = jnp.pad(w_in_f32[i, :, d_main:], ((0, 0), (0, V7X_LANES - rank))).astype(MXU_DTYPE)
    w_up = jnp.pad(w_gate_up, ((0, V7X_LANES - rank), (0, 0))).astype(MXU_DTYPE)
    u, g_low = _norm_matmul(x, g, w_in, i, d_main, w_tail=w_low, name="odd_in_proj")
    o = _gla_core(u.reshape(bsz, seq, -1), g_low.reshape(bsz, seq, -1), w_up, b_gate_up, head_g)
    return _matmul_residual(x, o.reshape(bsz * seq, -1), w_out, i, name="odd_out_proj")


def kernel(x, norm_g, ffn_pre_w1, ffn_pre_w3, ffn_pre_w2, ffn_post_w1, ffn_post_w3, ffn_post_w2, even_w_in,
           even_conv_w, even_sinks, even_w_out, odd_w_in, odd_w_gate_up, odd_b_gate_up, odd_head_g, odd_w_out,
           final_g):
    bsz, seq, d = x.shape
    depth = norm_g.shape[0]
    h = x.reshape(bsz * seq, d)
    pre = [w.astype(MXU_DTYPE) for w in (ffn_pre_w1, ffn_pre_w3, ffn_pre_w2)]
    post = [w.astype(MXU_DTYPE) for w in (ffn_post_w1, ffn_post_w3, ffn_post_w2)]
    even_in, even_out = even_w_in.astype(MXU_DTYPE), even_w_out.astype(MXU_DTYPE)
    odd_in, odd_out = odd_w_in.astype(MXU_DTYPE), odd_w_out.astype(MXU_DTYPE)
    for layer in range(depth):
        h = _ffn(h, norm_g[layer, 0], *pre, layer, final_g, final_norm=False)
        i = layer // 2
        if layer % 2 == 0:
            h = _even_mixer(h, norm_g[layer, 1], even_in, even_conv_w[i], even_sinks[i], even_out, i, bsz, seq)
        else:
            h = _odd_mixer(h, norm_g[layer, 1], odd_in, odd_w_in, odd_w_gate_up[i], odd_b_gate_up[i],
                           odd_head_g[i], odd_out, i, bsz, seq)
        h = _ffn(h, norm_g[layer, 2], *post, layer, final_g, final_norm=(layer == depth - 1))
    return h.reshape(bsz, seq, d)
```

```python
import functools

import jax
import jax.numpy as jnp
import numpy as np
from jax import lax
from jax.experimental import pallas as pl
from jax.experimental.pallas import tpu as pltpu

F32 = jnp.float32
MXU_DTYPE = jnp.bfloat16

NORM_EPS = 1e-6
FFN_HALF = 0.5
CONV_WIDTH = 3
HEAD_DIM = 64
N_KV_HEADS = 4
WINDOW = 128
ATTN_BLOCK = 128
GLA_HEADS = 4
GLA_CHUNK = 128
GLA_GATE_NORMALIZER = 16.0
ATTN_LOOKAHEAD = 2

V7X_LANES = 128
V7X_SUBLANES = 8
V7X_MXU_DIM = 256
V7X_VMEM_BYTES = 64 * 1024 * 1024


def _vmem_limit(block_bytes, scratch_bytes, temp_bytes):
    need = 2 * block_bytes + scratch_bytes + temp_bytes
    return int(min(need, V7X_VMEM_BYTES))


def _nbytes(shape, dtype):
    return int(np.prod(shape)) * jnp.dtype(dtype).itemsize


def _tile(n, want):
    if n <= want:
        return n
    t = want
    while n % t:
        t -= V7X_SUBLANES
    return t


def _rms_norm(x, g):
    return x * lax.rsqrt(jnp.mean(x * x, axis=-1, keepdims=True) + NORM_EPS) * g


def _dot_nt(a, b):
    return lax.dot_general(a, b, (((1,), (1,)), ((), ())), preferred_element_type=F32)


def _ffn_body(x_ref, g_ref, w1_ref, w3_ref, w2_ref, fg_ref, o_ref, n_ref, *, final_norm):
    k = pl.program_id(1)

    @pl.when(k == 0)
    def _():
        x = x_ref[...]
        n_ref[...] = _rms_norm(x, g_ref[...]).astype(n_ref.dtype)
        o_ref[...] = x

    n = n_ref[...]
    a = jnp.dot(n, w1_ref[...], preferred_element_type=F32)
    b = jnp.dot(n, w3_ref[...], preferred_element_type=F32)
    h = (a * jax.nn.sigmoid(a)) * b
    o_ref[...] += FFN_HALF * jnp.dot(h.astype(w2_ref.dtype), w2_ref[...], preferred_element_type=F32)

    if final_norm:
        @pl.when(k == pl.num_programs(1) - 1)
        def _():
            o_ref[...] = _rms_norm(o_ref[...], fg_ref[...])


def _ffn(x, g, w1, w3, w2, layer, final_g, *, final_norm):
    n_tok, d = x.shape
    d_ff = w1.shape[2]
    tm = _tile(n_tok, 1024)
    tf = _tile(d_ff, 512)
    blocks = (2 * _nbytes((tm, d), F32) + 2 * _nbytes((d, tf), MXU_DTYPE) + _nbytes((tf, d), MXU_DTYPE)
              + 2 * _nbytes((1, d), F32))
    scratch = _nbytes((tm, d), MXU_DTYPE)
    temps = 6 * _nbytes((tm, tf), F32)
    return pl.pallas_call(
        functools.partial(_ffn_body, final_norm=final_norm),
        out_shape=jax.ShapeDtypeStruct((n_tok, d), F32),
        grid=(n_tok // tm, d_ff // tf),
        in_specs=[
            pl.BlockSpec((tm, d), lambda i, k: (i, 0)),
            pl.BlockSpec((1, d), lambda i, k: (0, 0)),
            pl.BlockSpec((None, d, tf), lambda i, k: (layer, 0, k)),
            pl.BlockSpec((None, d, tf), lambda i, k: (layer, 0, k)),
            pl.BlockSpec((None, tf, d), lambda i, k: (layer, k, 0)),
            pl.BlockSpec((1, d), lambda i, k: (0, 0)),
        ],
        out_specs=pl.BlockSpec((tm, d), lambda i, k: (i, 0)),
        scratch_shapes=[pltpu.VMEM((tm, d), MXU_DTYPE)],
        compiler_params=pltpu.CompilerParams(
            dimension_semantics=("parallel", "arbitrary"),
            vmem_limit_bytes=_vmem_limit(blocks, scratch, temps)),
        name="ffn_final" if final_norm else "ffn",
    )(x, g.reshape(1, d), w1, w3, w2, final_g.reshape(1, d))


def _norm_matmul_body(x_ref, g_ref, w_ref, o_ref, n_ref):
    @pl.when(pl.program_id(1) == 0)
    def _():
        n_ref[...] = _rms_norm(x_ref[...], g_ref[...]).astype(n_ref.dtype)

    o_ref[...] = jnp.dot(n_ref[...], w_ref[...], preferred_element_type=F32)


def _norm_matmul_tail_body(x_ref, g_ref, w_ref, wt_ref, o_ref, ot_ref, n_ref):
    @pl.when(pl.program_id(1) == 0)
    def _():
        n_ref[...] = _rms_norm(x_ref[...], g_ref[...]).astype(n_ref.dtype)
        ot_ref[...] = jnp.dot(n_ref[...], wt_ref[...], preferred_element_type=F32)

    o_ref[...] = jnp.dot(n_ref[...], w_ref[...], preferred_element_type=F32)


def _norm_matmul(x, g, w, layer, n_out, *, w_tail=None, name):
    n_tok, d = x.shape
    tm = _tile(n_tok, 1024)
    tn = _tile(n_out, 1536)
    blocks = _nbytes((tm, d), F32) + _nbytes((1, d), F32) + _nbytes((d, tn), MXU_DTYPE) + _nbytes((tm, tn), F32)
    scratch = _nbytes((tm, d), MXU_DTYPE)
    temps = _nbytes((tm, tn), F32)
    in_specs = [
        pl.BlockSpec((tm, d), lambda i, j: (i, 0)),
        pl.BlockSpec((1, d), lambda i, j: (0, 0)),
        pl.BlockSpec((None, d, tn), lambda i, j: (layer, 0, j)),
    ]
    out_shape = jax.ShapeDtypeStruct((n_tok, n_out), F32)
    out_specs = pl.BlockSpec((tm, tn), lambda i, j: (i, j))
    args = (x, g.reshape(1, d), w)
    body = _norm_matmul_body
    if w_tail is not None:
        n_tail = w_tail.shape[1]
        blocks += _nbytes((d, n_tail), MXU_DTYPE) + _nbytes((tm, n_tail), F32)
        in_specs.append(pl.BlockSpec((d, n_tail), lambda i, j: (0, 0)))
        out_shape = (out_shape, jax.ShapeDtypeStruct((n_tok, n_tail), F32))
        out_specs = (out_specs, pl.BlockSpec((tm, n_tail), lambda i, j: (i, 0)))
        args = args + (w_tail,)
        body = _norm_matmul_tail_body
    return pl.pallas_call(
        body,
        out_shape=out_shape,
        grid=(n_tok // tm, n_out // tn),
        in_specs=in_specs,
        out_specs=out_specs,
        scratch_shapes=[pltpu.VMEM((tm, d), MXU_DTYPE)],
        compiler_params=pltpu.CompilerParams(
            dimension_semantics=("parallel", "arbitrary"),
            vmem_limit_bytes=_vmem_limit(blocks, scratch, temps)),
        name=name,
    )(*args)


def _matmul_residual_body(x_ref, y_ref, w_ref, o_ref):
    o_ref[...] = x_ref[...] + jnp.dot(y_ref[...], w_ref[...], preferred_element_type=F32)


def _matmul_residual(x, y, w, layer, *, name):
    n_tok, d = x.shape
    d_in = y.shape[1]
    tm = _tile(n_tok, 1024)
    tn = _tile(d, 1024)
    blocks = (2 * _nbytes((tm, tn), F32) + _nbytes((tm, d_in), y.dtype) + _nbytes((d_in, tn), MXU_DTYPE))
    temps = _nbytes((tm, tn), F32)
    return pl.pallas_call(
        _matmul_residual_body,
        out_shape=jax.ShapeDtypeStruct((n_tok, d), F32),
        grid=(n_tok // tm, d // tn),
        in_specs=[
            pl.BlockSpec((tm, tn), lambda i, j: (i, j)),
            pl.BlockSpec((tm, d_in), lambda i, j: (i, 0)),
            pl.BlockSpec((None, d_in, tn), lambda i, j: (layer, 0, j)),
        ],
        out_specs=pl.BlockSpec((tm, tn), lambda i, j: (i, j)),
        compiler_params=pltpu.CompilerParams(
            dimension_semantics=("parallel", "arbitrary"),
            vmem_limit_bytes=_vmem_limit(blocks, 0, temps)),
        name=name,
    )(x, y, w)


def _alibi_slopes(n_heads):
    ex = np.arange(1, n_heads + 1, dtype=np.float32) * np.float32(8.0 / n_heads)
    return np.power(np.float32(2.0), -ex)


def _even_core_body(sinks_ref, ub_ref, uc_ref, ux_ref, q_ref, kv_ref, kvp_ref, ucp_ref, uxp_ref, cw_ref,
                    o_ref, hs_ref, ks_ref, vts_ref, bias_ref, *, slopes):
    tq = ub_ref.shape[1]
    d_conv = ub_ref.shape[2]
    halo = ucp_ref.shape[1]
    blk = ATTN_BLOCK
    first_tile = pl.program_id(1) == 0
    n_heads = bias_ref.shape[0]

    key = lax.broadcasted_iota(jnp.int32, (2 * blk, blk), 0)
    qry = lax.broadcasted_iota(jnp.int32, (2 * blk, blk), 1)

    @pl.when((pl.program_id(0) == 0) & first_tile)
    def _():
        dist_i = qry + blk - key
        dist = dist_i.astype(F32)
        valid = (dist_i >= 0) & (dist_i < WINDOW)
        for head in range(n_heads):
            bias_ref[head] = jnp.where(valid, -(float(slopes[head]) * dist), -jnp.inf)

    prev = ucp_ref[0] * uxp_ref[0]
    hs_ref[0:halo, :] = jnp.where(first_tile, jnp.zeros_like(prev), prev)
    hs_ref[halo:, :] = uc_ref[0] * ux_ref[0]
    y = None
    for j in range(CONV_WIDTH):
        off = halo - (CONV_WIDTH - 1) + j
        term = cw_ref[j:j + 1, :] * hs_ref[off:off + tq, :]
        y = term if y is None else y + term
    o_ref[0, :, 0:d_conv] = (ub_ref[0] * y).astype(o_ref.dtype)

    half = V7X_LANES // 2
    low = lax.broadcasted_iota(jnp.int32, (tq + blk, V7X_LANES), 1) < half
    top = lax.broadcasted_iota(jnp.int32, (V7X_LANES, tq + blk), 0) < half
    kv_all = jnp.concatenate([kvp_ref[0], kv_ref[0]], axis=0)
    n_slabs = kv_all.shape[1] // (2 * V7X_LANES)
    for s in range(n_slabs):
        slab = kv_all[:, s * V7X_LANES:(s + 1) * V7X_LANES]
        swapped = pltpu.roll(slab, half, axis=1)
        zero = jnp.zeros_like(slab)
        ks_ref[2 * s, 0] = jnp.where(low, slab, zero).astype(ks_ref.dtype)
        ks_ref[2 * s, 1] = jnp.where(low, zero, swapped).astype(ks_ref.dtype)
        ks_ref[2 * s + 1, 0] = jnp.where(low, swapped, zero).astype(ks_ref.dtype)
        ks_ref[2 * s + 1, 1] = jnp.where(low, zero, slab).astype(ks_ref.dtype)
        slab_t = kv_all[:, (n_slabs + s) * V7X_LANES:(n_slabs + s + 1) * V7X_LANES].T
        swapped_t = pltpu.roll(slab_t, half, axis=0)
        zero_t = jnp.zeros_like(slab_t)
        vts_ref[2 * s, 0] = jnp.where(top, slab_t, zero_t).astype(vts_ref.dtype)
        vts_ref[2 * s, 1] = jnp.where(top, zero_t, swapped_t).astype(vts_ref.dtype)
        vts_ref[2 * s + 1, 0] = jnp.where(top, swapped_t, zero_t).astype(vts_ref.dtype)
        vts_ref[2 * s + 1, 1] = jnp.where(top, zero_t, slab_t).astype(vts_ref.dtype)

    no_history = first_tile & (key < blk)
    scale = HEAD_DIM ** -0.5
    group = n_heads // N_KV_HEADS

    def scores(qb, g):
        rows = slice(qb * blk, (qb + 1) * blk)
        keys = slice(qb * blk, (qb + 2) * blk)
        slab0 = g * group // 2
        q_slabs = [q_ref[0, rows, (slab0 + a) * V7X_LANES:(slab0 + a + 1) * V7X_LANES]
                   for a in range(group // 2)]
        q4 = (jnp.concatenate(q_slabs, axis=0) * scale).astype(MXU_DTYPE)
        k_cat = jnp.concatenate([ks_ref[g, 0, keys, :], ks_ref[g, 1, keys, :]], axis=0)
        return _dot_nt(k_cat, q4)

    def attend(qb, g, st):
        rows = slice(qb * blk, (qb + 1) * blk)
        keys = slice(qb * blk, (qb + 2) * blk)
        slab0 = g * group // 2
        p_cols = []
        for a in range(group // 2):
            p_rows = []
            for hi in range(2):
                head = g * group + 2 * a + hi
                sq = st[hi * 2 * blk:(hi + 1) * 2 * blk, a * blk:(a + 1) * blk] + bias_ref[head]
                if qb == 0:
                    sq = jnp.where(no_history, -jnp.inf, sq)
                sink = sinks_ref[head]
                m = jnp.maximum(jnp.max(sq, axis=0, keepdims=True), sink)
                e = jnp.exp(sq - m)
                denom = jnp.sum(e, axis=0, keepdims=True) + jnp.exp(sink - m)
                p_rows.append((e * (1.0 / denom)).astype(MXU_DTYPE))
            p_cols.append(jnp.concatenate(p_rows, axis=0))
        pt = jnp.concatenate(p_cols, axis=1)
        vt_cat = jnp.concatenate([vts_ref[g, 0, :, keys], vts_ref[g, 1, :, keys]], axis=1)
        out = jnp.dot(vt_cat, pt, preferred_element_type=F32).T
        for a in range(group // 2):
            c0 = d_conv + (slab0 + a) * V7X_LANES
            o_ref[0, rows, c0:c0 + V7X_LANES] = out[a * blk:(a + 1) * blk].astype(o_ref.dtype)

    work = [(qb, g) for qb in range(tq // blk) for g in range(N_KV_HEADS)]
    pending = [scores(*w) for w in work[:ATTN_LOOKAHEAD]]
    for i, w in enumerate(work):
        if i + ATTN_LOOKAHEAD < len(work):
            pending.append(scores(*work[i + ATTN_LOOKAHEAD]))
        attend(*w, pending.pop(0))


def _even_core(u, conv_w, sinks):
    bsz, seq, _ = u.shape
    d_conv = conv_w.shape[1]
    n_q_heads = sinks.shape[0]
    d_attn = n_q_heads * HEAD_DIM
    d_kv = N_KV_HEADS * HEAD_DIM
    assert d_conv == d_attn and (n_q_heads // N_KV_HEADS) % 2 == 0 and (2 * d_kv) % V7X_LANES == 0
    tq = _tile(seq, 512)
    assert tq % ATTN_BLOCK == 0
    halo = V7X_SUBLANES
    blocks_per_tile = tq // ATTN_BLOCK
    halos_per_tile = tq // halo
    kv_col = (3 * d_conv + d_attn) // (2 * d_kv)
    slopes = _alibi_slopes(n_q_heads)

    def prev_block(i, per_tile):
        return jnp.maximum(i * per_tile - 1, 0)

    blocks = (4 * _nbytes((tq, d_conv), F32) + _nbytes((tq + ATTN_BLOCK, 2 * d_kv), F32)
              + 2 * _nbytes((halo, d_conv), F32) + _nbytes((tq, d_conv + d_attn), MXU_DTYPE))
    scratch = (_nbytes((tq + halo, d_conv), F32) + 4 * _nbytes((tq + ATTN_BLOCK, 2 * d_kv), MXU_DTYPE)
               + _nbytes((n_q_heads, ATTN_BLOCK, 2 * ATTN_BLOCK), F32))
    temps = 3 * _nbytes((tq, d_conv), F32) + 2 * _nbytes((tq + ATTN_BLOCK, 2 * d_kv), F32)
    return pl.pallas_call(
        functools.partial(_even_core_body, slopes=slopes),
        out_shape=jax.ShapeDtypeStruct((bsz, seq, d_conv + d_attn), MXU_DTYPE),
        grid=(bsz, seq // tq),
        in_specs=[
            pl.BlockSpec(memory_space=pltpu.SMEM),
            pl.BlockSpec((1, tq, d_conv), lambda b, i: (b, i, 0)),
            pl.BlockSpec((1, tq, d_conv), lambda b, i: (b, i, 1)),
            pl.BlockSpec((1, tq, d_conv), lambda b, i: (b, i, 2)),
            pl.BlockSpec((1, tq, d_attn), lambda b, i: (b, i, 3)),
            pl.BlockSpec((1, tq, 2 * d_kv), lambda b, i: (b, i, kv_col)),
            pl.BlockSpec((1, ATTN_BLOCK, 2 * d_kv), lambda b, i: (b, prev_block(i, blocks_per_tile), kv_col)),
            pl.BlockSpec((1, halo, d_conv), lambda b, i: (b, prev_block(i, halos_per_tile), 1)),
            pl.BlockSpec((1, halo, d_conv), lambda b, i: (b, prev_block(i, halos_per_tile), 2)),
            pl.BlockSpec((CONV_WIDTH, d_conv), lambda b, i: (0, 0)),
        ],
        out_specs=pl.BlockSpec((1, tq, d_conv + d_attn), lambda b, i: (b, i, 0)),
        scratch_shapes=[
            pltpu.VMEM((tq + halo, d_conv), F32),
            pltpu.VMEM((N_KV_HEADS, 2, tq + ATTN_BLOCK, V7X_LANES), MXU_DTYPE),
            pltpu.VMEM((N_KV_HEADS, 2, V7X_LANES, tq + ATTN_BLOCK), MXU_DTYPE),
            pltpu.VMEM((n_q_heads, 2 * ATTN_BLOCK, ATTN_BLOCK), F32),
        ],
        compiler_params=pltpu.CompilerParams(
            dimension_semantics=("arbitrary", "arbitrary"),
            vmem_limit_bytes=_vmem_limit(blocks, scratch, temps)),
        name="even_core",
    )(sinks, u, u, u, u, u, u, u, u, conv_w)


def _split3(x):
    hi = x.astype(MXU_DTYPE)
    r1 = x - hi.astype(F32)
    mid = r1.astype(MXU_DTYPE)
    lo = (r1 - mid.astype(F32)).astype(MXU_DTYPE)
    return hi, mid, lo


def _gla_head(q, k, v, r, z, head_g, keep, tri, s_ref):
    tt, hk = q.shape
    ch = GLA_CHUNK
    n_chunks = tt // ch
    sub = keep.shape[0]

    log_a = (jnp.minimum(z, 0.0) - jnp.log(1.0 + jnp.exp(-jnp.abs(z)))) * (1.0 / GLA_GATE_NORMALIZER)

    parts = jnp.concatenate(_split3(log_a), axis=1)
    b = []
    for a in range(0, tt, sub):
        c3 = jnp.dot(tri, parts[a:a + sub], preferred_element_type=F32)
        b.append(c3[:, 0:hk] + c3[:, hk:2 * hk] + c3[:, 2 * hk:3 * hk])
    b = jnp.concatenate(b, axis=0)
    b_last = jnp.concatenate(
        [jnp.broadcast_to(b[(c + 1) * ch - 1:(c + 1) * ch, :], (ch, hk)) for c in range(n_chunks)], axis=0)

    q_dec = ((q * (hk ** -0.5)) * jnp.exp(b)).astype(MXU_DTYPE)
    k_inv = (k * jnp.exp(-b)).astype(MXU_DTYPE)
    k_tail_t = (k * jnp.exp(b_last - b)).T.astype(MXU_DTYPE)
    decay_t = jnp.exp(b_last.T)

    o_intra = []
    for a in range(0, tt, sub):
        attn = _dot_nt(q_dec[a:a + sub], k_inv[a:a + sub])
        attn = jnp.where(keep, attn, 0.0).astype(MXU_DTYPE)
        o_intra.append(jnp.dot(attn, v[a:a + sub], preferred_element_type=F32))
    o_intra = jnp.concatenate(o_intra, axis=0)

    o_inter = []
    for c in range(n_chunks):
        rows = slice(c * ch, (c + 1) * ch)
        state = s_ref[...]
        o_inter.append(jnp.dot(q_dec[rows], state.astype(MXU_DTYPE), preferred_element_type=F32))
        kv = jnp.dot(k_tail_t[:, rows], v[rows], preferred_element_type=F32)
        s_ref[...] = decay_t[:, c * ch:c * ch + 1] * state + kv
    o = o_intra + jnp.concatenate(o_inter, axis=0)

    o = _rms_norm(o, head_g)
    return o * (r * jax.nn.sigmoid(r))


def _gla_body(q_ref, k_ref, v_ref, r_ref, gl_ref, wgu_ref, bgu_ref, hg_ref, o_ref, s_ref):
    tt = q_ref.shape[1]
    n_heads = s_ref.shape[0]
    hk = q_ref.shape[2] // n_heads
    hv = v_ref.shape[2] // n_heads
    ch = GLA_CHUNK
    sub = min(tt, V7X_MXU_DIM)

    @pl.when(pl.program_id(1) == 0)
    def _():
        s_ref[...] = jnp.zeros_like(s_ref)

    ri = lax.broadcasted_iota(jnp.int32, (sub, sub), 0)
    ci = lax.broadcasted_iota(jnp.int32, (sub, sub), 1)
    keep = (ri // ch == ci // ch) & (ci <= ri)
    tri = jnp.where(keep, 1.0, 0.0).astype(MXU_DTYPE)

    g_low = gl_ref[0].astype(MXU_DTYPE)
    for h in range(n_heads):
        ks = slice(h * hk, (h + 1) * hk)
        vs = slice(h * hv, (h + 1) * hv)
        z = jnp.dot(g_low, wgu_ref[:, ks], preferred_element_type=F32) + bgu_ref[:, ks]
        o = _gla_head(q_ref[0, :, ks], k_ref[0, :, ks], v_ref[0, :, vs].astype(MXU_DTYPE), r_ref[0, :, vs], z,
                      hg_ref[...], keep, tri, s_ref.at[h])
        o_ref[0, :, vs] = o.astype(o_ref.dtype)


def _gla_core(u, g_low, w_gate_up, b_gate_up, head_g):
    bsz, seq, _ = u.shape
    gl_w, d_k = w_gate_up.shape
    hk = d_k // GLA_HEADS
    hv = head_g.shape[0]
    d_v = hv * GLA_HEADS
    assert d_v % d_k == 0
    tt = _tile(seq, 512)
    assert GLA_CHUNK % V7X_LANES == 0 and tt % GLA_CHUNK == 0 and min(tt, V7X_MXU_DIM) % GLA_CHUNK == 0
    blocks = (2 * _nbytes((tt, d_k), F32) + 2 * _nbytes((tt, d_v), F32) + _nbytes((tt, gl_w), F32)
              + _nbytes((gl_w, d_k), MXU_DTYPE) + _nbytes((1, d_k), F32) + _nbytes((1, hv), F32)
              + _nbytes((tt, d_v), MXU_DTYPE))
    scratch = _nbytes((GLA_HEADS, hk, hv), F32)
    temps = GLA_HEADS * (8 * _nbytes((tt, hk), F32) + 4 * _nbytes((tt, hv), F32))
    return pl.pallas_call(
        _gla_body,
        out_shape=jax.ShapeDtypeStruct((bsz, seq, d_v), MXU_DTYPE),
        grid=(bsz, seq // tt),
        in_specs=[
            pl.BlockSpec((1, tt, d_k), lambda b, t: (b, t, 0)),
            pl.BlockSpec((1, tt, d_k), lambda b, t: (b, t, 1)),
            pl.BlockSpec((1, tt, d_v), lambda b, t: (b, t, 2 * d_k // d_v)),
            pl.BlockSpec((1, tt, d_v), lambda b, t: (b, t, 2 * d_k // d_v + 1)),
            pl.BlockSpec((1, tt, gl_w), lambda b, t: (b, t, 0)),
            pl.BlockSpec((gl_w, d_k), lambda b, t: (0, 0)),
            pl.BlockSpec((1, d_k), lambda b, t: (0, 0)),
            pl.BlockSpec((1, hv), lambda b, t: (0, 0)),
        ],
        out_specs=pl.BlockSpec((1, tt, d_v), lambda b, t: (b, t, 0)),
        scratch_shapes=[pltpu.VMEM((GLA_HEADS, hk, hv), F32)],
        compiler_params=pltpu.CompilerParams(
            dimension_semantics=("parallel", "arbitrary"),
            vmem_limit_bytes=_vmem_limit(blocks, scratch, temps)),
        name="gla_core",
    )(u, u, u, u, g_low, w_gate_up, b_gate_up.reshape(1, d_k), head_g.reshape(1, hv))


def _even_mixer(x, g, w_in, conv_w, sinks, w_out, i, bsz, seq):
    u = _norm_matmul(x, g, w_in, i, w_in.shape[2], name="even_in_proj")
    y = _even_core(u.reshape(bsz, seq, -1), conv_w, sinks)
    return _matmul_residual(x, y.reshape(bsz * seq, -1), w_out, i, name="even_out_proj")


def _odd_mixer(x, g, w_in, w_in_f32, w_gate_up, b_gate_up, head_g, w_out, i, bsz, seq):
    rank, d_k = w_gate_up.shape
    d_main = w_in.shape[2] - rank
    w_low = jnp.pad(w_in_f32[i, :, d_main:], ((0, 0), (0, V7X_LANES - rank))).astype(MXU_DTYPE)
    w_up = jnp.pad(w_gate_up, ((0, V7X_LANES - rank), (0, 0))).astype(MXU_DTYPE)
    u, g_low = _norm_matmul(x, g, w_in, i, d_main, w_tail=w_low, name="odd_in_proj")
    o = _gla_core(u.reshape(bsz, seq, -1), g_low.reshape(bsz, seq, -1), w_up, b_gate_up, head_g)
    return _matmul_residual(x, o.reshape(bsz * seq, -1), w_out, i, name="odd_out_proj")


def kernel(x, norm_g, ffn_pre_w1, ffn_pre_w3, ffn_pre_w2, ffn_post_w1, ffn_post_w3, ffn_post_w2, even_w_in,
           even_conv_w, even_sinks, even_w_out, odd_w_in, odd_w_gate_up, odd_b_gate_up, odd_head_g, odd_w_out,
           final_g):
    bsz, seq, d = x.shape
    depth = norm_g.shape[0]
    h = x.reshape(bsz * seq, d)
    pre = [w.astype(MXU_DTYPE) for w in (ffn_pre_w1, ffn_pre_w3, ffn_pre_w2)]
    post = [w.astype(MXU_DTYPE) for w in (ffn_post_w1, ffn_post_w3, ffn_post_w2)]
    even_in, even_out = even_w_in.astype(MXU_DTYPE), even_w_out.astype(MXU_DTYPE)
    odd_in, odd_out = odd_w_in.astype(MXU_DTYPE), odd_w_out.astype(MXU_DTYPE)
    for layer in range(depth):
        h = _ffn(h, norm_g[layer, 0], *pre, layer, final_g, final_norm=False)
        i = layer // 2
        if layer % 2 == 0:
            h = _even_mixer(h, norm_g[layer, 1], even_in, even_conv_w[i], even_sinks[i], even_out, i, bsz, seq)
        else:
            h = _odd_mixer(h, norm_g[layer, 1], odd_in, odd_w_in, odd_w_gate_up[i], odd_b_gate_up[i],
                           odd_head_g[i], odd_out, i, bsz, seq)
        h = _ffn(h, norm_g[layer, 2], *post, layer, final_g, final_norm=(layer == depth - 1))
    return h.reshape(bsz, seq, d)
```

```python
import functools

import jax
import jax.numpy as jnp
import numpy as np
from jax import lax
from jax.experimental import pallas as pl
from jax.experimental.pallas import tpu as pltpu

F32 = jnp.float32
MXU_DTYPE = jnp.bfloat16

NORM_EPS = 1e-6
FFN_HALF = 0.5
CONV_WIDTH = 3
HEAD_DIM = 64
N_KV_HEADS = 4
WINDOW = 128
ATTN_BLOCK = 128
GLA_HEADS = 4
GLA_CHUNK = 128
GLA_GATE_NORMALIZER = 16.0
ATTN_LOOKAHEAD = 2

V7X_LANES = 128
V7X_SUBLANES = 8
V7X_MXU_DIM = 256
V7X_VMEM_BYTES = 64 * 1024 * 1024


def _vmem_limit(block_bytes, scratch_bytes, temp_bytes):
    need = 2 * block_bytes + scratch_bytes + temp_bytes
    return int(min(need, V7X_VMEM_BYTES))


def _nbytes(shape, dtype):
    return int(np.prod(shape)) * jnp.dtype(dtype).itemsize


def _tile(n, want):
    if n <= want:
        return n
    t = want
    while n % t:
        t -= V7X_SUBLANES
    return t


def _rms_norm(x, g):
    return x * lax.rsqrt(jnp.mean(x * x, axis=-1, keepdims=True) + NORM_EPS) * g


def _dot_nt(a, b):
    return lax.dot_general(a, b, (((1,), (1,)), ((), ())), preferred_element_type=F32)


def _ffn_body(x_ref, g_ref, w1_ref, w3_ref, w2_ref, fg_ref, o_ref, n_ref, *, final_norm):
    k = pl.program_id(1)

    def half_swiglu_chunk():
        n = n_ref[...]
        a = jnp.dot(n, w1_ref[...], preferred_element_type=F32)
        b = jnp.dot(n, w3_ref[...], preferred_element_type=F32)
        h = (a * jax.nn.sigmoid(a)) * b
        return FFN_HALF * jnp.dot(h.astype(w2_ref.dtype), w2_ref[...], preferred_element_type=F32)

    @pl.when(k == 0)
    def _():
        n_ref[...] = _rms_norm(x_ref[...], g_ref[...]).astype(n_ref.dtype)
        o_ref[...] = x_ref[...] + half_swiglu_chunk()

    @pl.when(k > 0)
    def _():
        o_ref[...] += half_swiglu_chunk()

    if final_norm:
        @pl.when(k == pl.num_programs(1) - 1)
        def _():
            o_ref[...] = _rms_norm(o_ref[...], fg_ref[...])


def _ffn(x, g, w1, w3, w2, layer, final_g, *, final_norm):
    n_tok, d = x.shape
    d_ff = w1.shape[2]
    tm = _tile(n_tok, 1024)
    tf = _tile(d_ff, 512)
    blocks = (2 * _nbytes((tm, d), F32) + 2 * _nbytes((d, tf), MXU_DTYPE) + _nbytes((tf, d), MXU_DTYPE)
              + 2 * _nbytes((1, d), F32))
    scratch = _nbytes((tm, d), MXU_DTYPE)
    temps = 6 * _nbytes((tm, tf), F32)
    return pl.pallas_call(
        functools.partial(_ffn_body, final_norm=final_norm),
        out_shape=jax.ShapeDtypeStruct((n_tok, d), F32),
        grid=(n_tok // tm, d_ff // tf),
        in_specs=[
            pl.BlockSpec((tm, d), lambda i, k: (i, 0)),
            pl.BlockSpec((1, d), lambda i, k: (0, 0)),
            pl.BlockSpec((None, d, tf), lambda i, k: (layer, 0, k)),
            pl.BlockSpec((None, d, tf), lambda i, k: (layer, 0, k)),
            pl.BlockSpec((None, tf, d), lambda i, k: (layer, k, 0)),
            pl.BlockSpec((1, d), lambda i, k: (0, 0)),
        ],
        out_specs=pl.BlockSpec((tm, d), lambda i, k: (i, 0)),
        scratch_shapes=[pltpu.VMEM((tm, d), MXU_DTYPE)],
        compiler_params=pltpu.CompilerParams(
            dimension_semantics=("parallel", "arbitrary"),
            vmem_limit_bytes=_vmem_limit(blocks, scratch, temps)),
        name="ffn_final" if final_norm else "ffn",
    )(x, g.reshape(1, d), w1, w3, w2, final_g.reshape(1, d))


def _norm_matmul_body(x_ref, g_ref, w_ref, o_ref, n_ref):
    @pl.when(pl.program_id(1) == 0)
    def _():
        n_ref[...] = _rms_norm(x_ref[...], g_ref[...]).astype(n_ref.dtype)

    o_ref[...] = jnp.dot(n_ref[...], w_ref[...], preferred_element_type=F32)


def _norm_matmul_tail_body(x_ref, g_ref, w_ref, wt_ref, o_ref, ot_ref, n_ref):
    @pl.when(pl.program_id(1) == 0)
    def _():
        n_ref[...] = _rms_norm(x_ref[...], g_ref[...]).astype(n_ref.dtype)
        ot_ref[...] = jnp.dot(n_ref[...], wt_ref[...], preferred_element_type=F32)

    o_ref[...] = jnp.dot(n_ref[...], w_ref[...], preferred_element_type=F32)


def _norm_matmul(x, g, w, layer, n_out, *, w_tail=None, name):
    n_tok, d = x.shape
    tm = _tile(n_tok, 1024)
    tn = _tile(n_out, 1536)
    blocks = _nbytes((tm, d), F32) + _nbytes((1, d), F32) + _nbytes((d, tn), MXU_DTYPE) + _nbytes((tm, tn), F32)
    scratch = _nbytes((tm, d), MXU_DTYPE)
    temps = _nbytes((tm, tn), F32)
    in_specs = [
        pl.BlockSpec((tm, d), lambda i, j: (i, 0)),
        pl.BlockSpec((1, d), lambda i, j: (0, 0)),
        pl.BlockSpec((None, d, tn), lambda i, j: (layer, 0, j)),
    ]
    out_shape = jax.ShapeDtypeStruct((n_tok, n_out), F32)
    out_specs = pl.BlockSpec((tm, tn), lambda i, j: (i, j))
    args = (x, g.reshape(1, d), w)
    body = _norm_matmul_body
    if w_tail is not None:
        n_tail = w_tail.shape[1]
        blocks += _nbytes((d, n_tail), MXU_DTYPE) + _nbytes((tm, n_tail), F32)
        in_specs.append(pl.BlockSpec((d, n_tail), lambda i, j: (0, 0)))
        out_shape = (out_shape, jax.ShapeDtypeStruct((n_tok, n_tail), F32))
        out_specs = (out_specs, pl.BlockSpec((tm, n_tail), lambda i, j: (i, 0)))
        args = args + (w_tail,)
        body = _norm_matmul_tail_body
    return pl.pallas_call(
        body,
        out_shape=out_shape,
        grid=(n_tok // tm, n_out // tn),
        in_specs=in_specs,
        out_specs=out_specs,
        scratch_shapes=[pltpu.VMEM((tm, d), MXU_DTYPE)],
        compiler_params=pltpu.CompilerParams(
            dimension_semantics=("parallel", "arbitrary"),
            vmem_limit_bytes=_vmem_limit(blocks, scratch, temps)),
        name=name,
    )(*args)


def _matmul_residual_body(x_ref, y_ref, w_ref, o_ref):
    o_ref[...] = x_ref[...] + jnp.dot(y_ref[...], w_ref[...], preferred_element_type=F32)


def _matmul_residual(x, y, w, layer, *, name):
    n_tok, d = x.shape
    d_in = y.shape[1]
    tm = _tile(n_tok, 1024)
    tn = _tile(d, 1024)
    blocks = (2 * _nbytes((tm, tn), F32) + _nbytes((tm, d_in), y.dtype) + _nbytes((d_in, tn), MXU_DTYPE))
    temps = _nbytes((tm, tn), F32)
    return pl.pallas_call(
        _matmul_residual_body,
        out_shape=jax.ShapeDtypeStruct((n_tok, d), F32),
        grid=(n_tok // tm, d // tn),
        in_specs=[
            pl.BlockSpec((tm, tn), lambda i, j: (i, j)),
            pl.BlockSpec((tm, d_in), lambda i, j: (i, 0)),
            pl.BlockSpec((None, d_in, tn), lambda i, j: (layer, 0, j)),
        ],
        out_specs=pl.BlockSpec((tm, tn), lambda i, j: (i, j)),
        compiler_params=pltpu.CompilerParams(
            dimension_semantics=("parallel", "arbitrary"),
            vmem_limit_bytes=_vmem_limit(blocks, 0, temps)),
        name=name,
    )(x, y, w)


def _alibi_slopes(n_heads):
    ex = np.arange(1, n_heads + 1, dtype=np.float32) * np.float32(8.0 / n_heads)
    return np.power(np.float32(2.0), -ex)


def _even_core_body(sinks_ref, ub_ref, uc_ref, ux_ref, q_ref, kv_ref, kvp_ref, ucp_ref, uxp_ref, cw_ref,
                    o_ref, hs_ref, ks_ref, vts_ref, bias_ref, *, slopes):
    tq = ub_ref.shape[1]
    d_conv = ub_ref.shape[2]
    halo = ucp_ref.shape[1]
    blk = ATTN_BLOCK
    first_tile = pl.program_id(1) == 0
    n_heads = bias_ref.shape[0]

    key = lax.broadcasted_iota(jnp.int32, (2 * blk, blk), 0)
    qry = lax.broadcasted_iota(jnp.int32, (2 * blk, blk), 1)

    @pl.when((pl.program_id(0) == 0) & first_tile)
    def _():
        dist_i = qry + blk - key
        dist = dist_i.astype(F32)
        valid = (dist_i >= 0) & (dist_i < WINDOW)
        for head in range(n_heads):
            bias_ref[head] = jnp.where(valid, -(float(slopes[head]) * dist), -jnp.inf)

    prev = ucp_ref[0] * uxp_ref[0]
    hs_ref[0:halo, :] = jnp.where(first_tile, jnp.zeros_like(prev), prev)
    hs_ref[halo:, :] = uc_ref[0] * ux_ref[0]
    y = None
    for j in range(CONV_WIDTH):
        off = halo - (CONV_WIDTH - 1) + j
        term = cw_ref[j:j + 1, :] * hs_ref[off:off + tq, :]
        y = term if y is None else y + term
    o_ref[0, :, 0:d_conv] = (ub_ref[0] * y).astype(o_ref.dtype)

    half = V7X_LANES // 2
    low = lax.broadcasted_iota(jnp.int32, (tq + blk, V7X_LANES), 1) < half
    top = lax.broadcasted_iota(jnp.int32, (V7X_LANES, tq + blk), 0) < half
    kv_all = jnp.concatenate([kvp_ref[0], kv_ref[0]], axis=0)
    n_slabs = kv_all.shape[1] // (2 * V7X_LANES)
    for s in range(n_slabs):
        slab = kv_all[:, s * V7X_LANES:(s + 1) * V7X_LANES]
        swapped = pltpu.roll(slab, half, axis=1)
        zero = jnp.zeros_like(slab)
        ks_ref[2 * s, 0] = jnp.where(low, slab, zero).astype(ks_ref.dtype)
        ks_ref[2 * s, 1] = jnp.where(low, zero, swapped).astype(ks_ref.dtype)
        ks_ref[2 * s + 1, 0] = jnp.where(low, swapped, zero).astype(ks_ref.dtype)
        ks_ref[2 * s + 1, 1] = jnp.where(low, zero, slab).astype(ks_ref.dtype)
        slab_t = kv_all[:, (n_slabs + s) * V7X_LANES:(n_slabs + s + 1) * V7X_LANES].T
        swapped_t = pltpu.roll(slab_t, half, axis=0)
        zero_t = jnp.zeros_like(slab_t)
        vts_ref[2 * s, 0] = jnp.where(top, slab_t, zero_t).astype(vts_ref.dtype)
        vts_ref[2 * s, 1] = jnp.where(top, zero_t, swapped_t).astype(vts_ref.dtype)
        vts_ref[2 * s + 1, 0] = jnp.where(top, swapped_t, zero_t).astype(vts_ref.dtype)
        vts_ref[2 * s + 1, 1] = jnp.where(top, zero_t, slab_t).astype(vts_ref.dtype)

    no_history = first_tile & (key < blk)
    scale = HEAD_DIM ** -0.5
    group = n_heads // N_KV_HEADS

    def scores(qb, g):
        rows = slice(qb * blk, (qb + 1) * blk)
        keys = slice(qb * blk, (qb + 2) * blk)
        slab0 = g * group // 2
        q_slabs = [q_ref[0, rows, (slab0 + a) * V7X_LANES:(slab0 + a + 1) * V7X_LANES]
                   for a in range(group // 2)]
        q4 = (jnp.concatenate(q_slabs, axis=0) * scale).astype(MXU_DTYPE)
        k_cat = jnp.concatenate([ks_ref[g, 0, keys, :], ks_ref[g, 1, keys, :]], axis=0)
        return _dot_nt(k_cat, q4)

    def attend(qb, g, st):
        rows = slice(qb * blk, (qb + 1) * blk)
        keys = slice(qb * blk, (qb + 2) * blk)
        slab0 = g * group // 2
        p_cols = []
        for a in range(group // 2):
            p_rows = []
            for hi in range(2):
                head = g * group + 2 * a + hi
                sq = st[hi * 2 * blk:(hi + 1) * 2 * blk, a * blk:(a + 1) * blk] + bias_ref[head]
                if qb == 0:
                    sq = jnp.where(no_history, -jnp.inf, sq)
                sink = sinks_ref[head]
                m = jnp.maximum(jnp.max(sq, axis=0, keepdims=True), sink)
                e = jnp.exp(sq - m)
                denom = jnp.sum(e, axis=0, keepdims=True) + jnp.exp(sink - m)
                p_rows.append((e * (1.0 / denom)).astype(MXU_DTYPE))
            p_cols.append(jnp.concatenate(p_rows, axis=0))
        pt = jnp.concatenate(p_cols, axis=1)
        vt_cat = jnp.concatenate([vts_ref[g, 0, :, keys], vts_ref[g, 1, :, keys]], axis=1)
        out = jnp.dot(vt_cat, pt, preferred_element_type=F32).T
        for a in range(group // 2):
            c0 = d_conv + (slab0 + a) * V7X_LANES
            o_ref[0, rows, c0:c0 + V7X_LANES] = out[a * blk:(a + 1) * blk].astype(o_ref.dtype)

    work = [(qb, g) for qb in range(tq // blk) for g in range(N_KV_HEADS)]
    pending = [scores(*w) for w in work[:ATTN_LOOKAHEAD]]
    for i, w in enumerate(work):
        if i + ATTN_LOOKAHEAD < len(work):
            pending.append(scores(*work[i + ATTN_LOOKAHEAD]))
        attend(*w, pending.pop(0))


def _even_core(u, conv_w, sinks):
    bsz, seq, _ = u.shape
    d_conv = conv_w.shape[1]
    n_q_heads = sinks.shape[0]
    d_attn = n_q_heads * HEAD_DIM
    d_kv = N_KV_HEADS * HEAD_DIM
    assert d_conv == d_attn and (n_q_heads // N_KV_HEADS) % 2 == 0 and (2 * d_kv) % V7X_LANES == 0
    tq = _tile(seq, 512)
    assert tq % ATTN_BLOCK == 0
    halo = V7X_SUBLANES
    blocks_per_tile = tq // ATTN_BLOCK
    halos_per_tile = tq // halo
    kv_col = (3 * d_conv + d_attn) // (2 * d_kv)
    slopes = _alibi_slopes(n_q_heads)

    def prev_block(i, per_tile):
        return jnp.maximum(i * per_tile - 1, 0)

    blocks = (4 * _nbytes((tq, d_conv), F32) + _nbytes((tq + ATTN_BLOCK, 2 * d_kv), F32)
              + 2 * _nbytes((halo, d_conv), F32) + _nbytes((tq, d_conv + d_attn), MXU_DTYPE))
    scratch = (_nbytes((tq + halo, d_conv), F32) + 4 * _nbytes((tq + ATTN_BLOCK, 2 * d_kv), MXU_DTYPE)
               + _nbytes((n_q_heads, ATTN_BLOCK, 2 * ATTN_BLOCK), F32))
    temps = 3 * _nbytes((tq, d_conv), F32) + 2 * _nbytes((tq + ATTN_BLOCK, 2 * d_kv), F32)
    return pl.pallas_call(
        functools.partial(_even_core_body, slopes=slopes),
        out_shape=jax.ShapeDtypeStruct((bsz, seq, d_conv + d_attn), MXU_DTYPE),
        grid=(bsz, seq // tq),
        in_specs=[
            pl.BlockSpec(memory_space=pltpu.SMEM),
            pl.BlockSpec((1, tq, d_conv), lambda b, i: (b, i, 0)),
            pl.BlockSpec((1, tq, d_conv), lambda b, i: (b, i, 1)),
            pl.BlockSpec((1, tq, d_conv), lambda b, i: (b, i, 2)),
            pl.BlockSpec((1, tq, d_attn), lambda b, i: (b, i, 3)),
            pl.BlockSpec((1, tq, 2 * d_kv), lambda b, i: (b, i, kv_col)),
            pl.BlockSpec((1, ATTN_BLOCK, 2 * d_kv), lambda b, i: (b, prev_block(i, blocks_per_tile), kv_col)),
            pl.BlockSpec((1, halo, d_conv), lambda b, i: (b, prev_block(i, halos_per_tile), 1)),
            pl.BlockSpec((1, halo, d_conv), lambda b, i: (b, prev_block(i, halos_per_tile), 2)),
            pl.BlockSpec((CONV_WIDTH, d_conv), lambda b, i: (0, 0)),
        ],
        out_specs=pl.BlockSpec((1, tq, d_conv + d_attn), lambda b, i: (b, i, 0)),
        scratch_shapes=[
            pltpu.VMEM((tq + halo, d_conv), F32),
            pltpu.VMEM((N_KV_HEADS, 2, tq + ATTN_BLOCK, V7X_LANES), MXU_DTYPE),
            pltpu.VMEM((N_KV_HEADS, 2, V7X_LANES, tq + ATTN_BLOCK), MXU_DTYPE),
            pltpu.VMEM((n_q_heads, 2 * ATTN_BLOCK, ATTN_BLOCK), F32),
        ],
        compiler_params=pltpu.CompilerParams(
            dimension_semantics=("arbitrary", "arbitrary"),
            vmem_limit_bytes=_vmem_limit(blocks, scratch, temps)),
        name="even_core",
    )(sinks, u, u, u, u, u, u, u, u, conv_w)


def _split3(x):
    hi = x.astype(MXU_DTYPE)
    r1 = x - hi.astype(F32)
    mid = r1.astype(MXU_DTYPE)
    lo = (r1 - mid.astype(F32)).astype(MXU_DTYPE)
    return hi, mid, lo


GLA_PROLOGUE_STAGES = 3


def _gla_head(load, store, g_low, keep, tri, s_ref):
    q, k, v, r, w_up, b_up, head_g = load()
    tt, hk = q.shape
    ch = GLA_CHUNK
    n_chunks = tt // ch
    sub = keep.shape[0]

    z = jnp.dot(g_low, w_up, preferred_element_type=F32) + b_up
    log_a = (jnp.minimum(z, 0.0) - jnp.log(1.0 + jnp.exp(-jnp.abs(z)))) * (1.0 / GLA_GATE_NORMALIZER)
    parts = jnp.concatenate(_split3(log_a), axis=1)
    yield

    b = []
    for a in range(0, tt, sub):
        c3 = jnp.dot(tri, parts[a:a + sub], preferred_element_type=F32)
        b.append(c3[:, 0:hk] + c3[:, hk:2 * hk] + c3[:, 2 * hk:3 * hk])
    b = jnp.concatenate(b, axis=0)
    b_last = jnp.concatenate(
        [jnp.broadcast_to(b[(c + 1) * ch - 1:(c + 1) * ch, :], (ch, hk)) for c in range(n_chunks)], axis=0)

    q_dec = ((q * (hk ** -0.5)) * jnp.exp(b)).astype(MXU_DTYPE)
    k_inv = (k * jnp.exp(-b)).astype(MXU_DTYPE)
    k_tail_t = (k * jnp.exp(b_last - b)).T.astype(MXU_DTYPE)
    decay_t = jnp.exp(b_last.T)
    yield

    o_intra = []
    for a in range(0, tt, sub):
        attn = _dot_nt(q_dec[a:a + sub], k_inv[a:a + sub])
        attn = jnp.where(keep, attn, 0.0).astype(MXU_DTYPE)
        o_intra.append(jnp.dot(attn, v[a:a + sub], preferred_element_type=F32))
    o_intra = jnp.concatenate(o_intra, axis=0)
    yield

    o_inter = []
    for c in range(n_chunks):
        rows = slice(c * ch, (c + 1) * ch)
        state = s_ref[...]
        o_inter.append(jnp.dot(q_dec[rows], state.astype(MXU_DTYPE), preferred_element_type=F32))
        kv = jnp.dot(k_tail_t[:, rows], v[rows], preferred_element_type=F32)
        s_ref[...] = decay_t[:, c * ch:c * ch + 1] * state + kv
        yield
    o = o_intra + jnp.concatenate(o_inter, axis=0)

    o = _rms_norm(o, head_g)
    store(o * (r * jax.nn.sigmoid(r)))


def _gla_body(q_ref, k_ref, v_ref, r_ref, gl_ref, wgu_ref, bgu_ref, hg_ref, o_ref, s_ref):
    tt = q_ref.shape[1]
    n_heads = s_ref.shape[0]
    hk = q_ref.shape[2] // n_heads
    hv = v_ref.shape[2] // n_heads
    ch = GLA_CHUNK
    sub = min(tt, V7X_MXU_DIM)

    @pl.when(pl.program_id(1) == 0)
    def _():
        s_ref[...] = jnp.zeros_like(s_ref)

    ri = lax.broadcasted_iota(jnp.int32, (sub, sub), 0)
    ci = lax.broadcasted_iota(jnp.int32, (sub, sub), 1)
    keep = (ri // ch == ci // ch) & (ci <= ri)
    tri = jnp.where(keep, 1.0, 0.0).astype(MXU_DTYPE)

    g_low = gl_ref[0].astype(MXU_DTYPE)

    def head(h):
        ks = slice(h * hk, (h + 1) * hk)
        vs = slice(h * hv, (h + 1) * hv)

        def load():
            return (q_ref[0, :, ks], k_ref[0, :, ks], v_ref[0, :, vs].astype(MXU_DTYPE), r_ref[0, :, vs],
                    wgu_ref[:, ks], bgu_ref[:, ks], hg_ref[...])

        def store(o):
            o_ref[0, :, vs] = o.astype(o_ref.dtype)

        return _gla_head(load, store, g_low, keep, tri, s_ref.at[h])

    heads = [head(h) for h in range(n_heads)]
    finished = object()
    for _ in range(GLA_PROLOGUE_STAGES):
        next(heads[0])
    for h in range(n_heads):
        ahead = heads[h + 1] if h + 1 < n_heads else None
        todo = GLA_PROLOGUE_STAGES if ahead is not None else 0
        for _ in range(tt // ch):
            next(heads[h])
            if todo:
                next(ahead)
                todo -= 1
        assert next(heads[h], finished) is finished
        for _ in range(todo):
            next(ahead)


def _gla_core(u, g_low, w_gate_up, b_gate_up, head_g):
    bsz, seq, _ = u.shape
    gl_w, d_k = w_gate_up.shape
    hk = d_k // GLA_HEADS
    hv = head_g.shape[0]
    d_v = hv * GLA_HEADS
    assert d_v % d_k == 0
    tt = _tile(seq, 512)
    assert GLA_CHUNK % V7X_LANES == 0 and tt % GLA_CHUNK == 0 and min(tt, V7X_MXU_DIM) % GLA_CHUNK == 0
    blocks = (2 * _nbytes((tt, d_k), F32) + 2 * _nbytes((tt, d_v), F32) + _nbytes((tt, gl_w), F32)
              + _nbytes((gl_w, d_k), MXU_DTYPE) + _nbytes((1, d_k), F32) + _nbytes((1, hv), F32)
              + _nbytes((tt, d_v), MXU_DTYPE))
    scratch = _nbytes((GLA_HEADS, hk, hv), F32)
    temps = GLA_HEADS * (8 * _nbytes((tt, hk), F32) + 4 * _nbytes((tt, hv), F32))
    return pl.pallas_call(
        _gla_body,
        out_shape=jax.ShapeDtypeStruct((bsz, seq, d_v), MXU_DTYPE),
        grid=(bsz, seq // tt),
        in_specs=[
            pl.BlockSpec((1, tt, d_k), lambda b, t: (b, t, 0)),
            pl.BlockSpec((1, tt, d_k), lambda b, t: (b, t, 1)),
            pl.BlockSpec((1, tt, d_v), lambda b, t: (b, t, 2 * d_k // d_v)),
            pl.BlockSpec((1, tt, d_v), lambda b, t: (b, t, 2 * d_k // d_v + 1)),
            pl.BlockSpec((1, tt, gl_w), lambda b, t: (b, t, 0)),
            pl.BlockSpec((gl_w, d_k), lambda b, t: (0, 0)),
            pl.BlockSpec((1, d_k), lambda b, t: (0, 0)),
            pl.BlockSpec((1, hv), lambda b, t: (0, 0)),
        ],
        out_specs=pl.BlockSpec((1, tt, d_v), lambda b, t: (b, t, 0)),
        scratch_shapes=[pltpu.VMEM((GLA_HEADS, hk, hv), F32)],
        compiler_params=pltpu.CompilerParams(
            dimension_semantics=("parallel", "arbitrary"),
            vmem_limit_bytes=_vmem_limit(blocks, scratch, temps)),
        name="gla_core",
    )(u, u, u, u, g_low, w_gate_up, b_gate_up.reshape(1, d_k), head_g.reshape(1, hv))


def _even_mixer(x, g, w_in, conv_w, sinks, w_out, i, bsz, seq):
    u = _norm_matmul(x, g, w_in, i, w_in.shape[2], name="even_in_proj")
    y = _even_core(u.reshape(bsz, seq, -1), conv_w, sinks)
    return _matmul_residual(x, y.reshape(bsz * seq, -1), w_out, i, name="even_out_proj")


def _odd_mixer(x, g, w_in, w_in_f32, w_gate_up, b_gate_up, head_g, w_out, i, bsz, seq):
    rank, d_k = w_gate_up.shape
    d_main = w_in.shape[2] - rank
    w_low = jnp.pad(w_in_f32[i, :, d_main:], ((0, 0), (0, V7X_LANES - rank))).astype(MXU_DTYPE)
    w_up = jnp.pad(w_gate_up, ((0, V7X_LANES - rank), (0, 0))).astype(MXU_DTYPE)
    u, g_low = _norm_matmul(x, g, w_in, i, d_main, w_tail=w_low, name="odd_in_proj")
    o = _gla_core(u.reshape(bsz, seq, -1), g_low.reshape(bsz, seq, -1), w_up, b_gate_up, head_g)
    return _matmul_residual(x, o.reshape(bsz * seq, -1), w_out, i, name="odd_out_proj")


def kernel(x, norm_g, ffn_pre_w1, ffn_pre_w3, ffn_pre_w2, ffn_post_w1, ffn_post_w3, ffn_post_w2, even_w_in,
           even_conv_w, even_sinks, even_w_out, odd_w_in, odd_w_gate_up, odd_b_gate_up, odd_head_g, odd_w_out,
           final_g):
    bsz, seq, d = x.shape
    depth = norm_g.shape[0]
    h = x.reshape(bsz * seq, d)
    pre = [w.astype(MXU_DTYPE) for w in (ffn_pre_w1, ffn_pre_w3, ffn_pre_w2)]
    post = [w.astype(MXU_DTYPE) for w in (ffn_post_w1, ffn_post_w3, ffn_post_w2)]
    even_in, even_out = even_w_in.astype(MXU_DTYPE), even_w_out.astype(MXU_DTYPE)
    odd_in, odd_out = odd_w_in.astype(MXU_DTYPE), odd_w_out.astype(MXU_DTYPE)
    for layer in range(depth):
        h = _ffn(h, norm_g[layer, 0], *pre, layer, final_g, final_norm=False)
        i = layer // 2
        if layer % 2 == 0:
            h = _even_mixer(h, norm_g[layer, 1], even_in, even_conv_w[i], even_sinks[i], even_out, i, bsz, seq)
        else:
            h = _odd_mixer(h, norm_g[layer, 1], odd_in, odd_w_in, odd_w_gate_up[i], odd_b_gate_up[i],
                           odd_head_g[i], odd_out, i, bsz, seq)
        h = _ffn(h, norm_g[layer, 2], *post, layer, final_g, final_norm=(layer == depth - 1))
    return h.reshape(bsz, seq, d)
```

```python
import functools

import jax
import jax.numpy as jnp
import numpy as np
from jax import lax
from jax.experimental import pallas as pl
from jax.experimental.pallas import tpu as pltpu

F32 = jnp.float32
MXU_DTYPE = jnp.bfloat16

NORM_EPS = 1e-6
FFN_HALF = 0.5
CONV_WIDTH = 3
HEAD_DIM = 64
N_KV_HEADS = 4
WINDOW = 128
ATTN_BLOCK = 128
GLA_HEADS = 4
GLA_CHUNK = 128
GLA_GATE_NORMALIZER = 16.0
ATTN_LOOKAHEAD = 6
PROJECT_BEFORE_STAGING = 2

V7X_LANES = 128
V7X_SUBLANES = 8
V7X_MXU_DIM = 256
V7X_VMEM_BYTES = 64 * 1024 * 1024


def _vmem_limit(block_bytes, scratch_bytes, temp_bytes):
    need = 2 * block_bytes + scratch_bytes + temp_bytes
    return int(min(need, V7X_VMEM_BYTES))


def _nbytes(shape, dtype):
    return int(np.prod(shape)) * jnp.dtype(dtype).itemsize


def _tile(n, want):
    if n <= want:
        return n
    t = want
    while n % t:
        t -= V7X_SUBLANES
    return t


def _rms_norm(x, g):
    return x * lax.rsqrt(jnp.mean(x * x, axis=-1, keepdims=True) + NORM_EPS) * g


def _dot_nt(a, b):
    return lax.dot_general(a, b, (((1,), (1,)), ((), ())), preferred_element_type=F32)


def _ffn_body(x_ref, g_ref, w1_ref, w3_ref, w2_ref, fg_ref, o_ref, n_ref, *, final_norm):
    k = pl.program_id(1)

    def half_swiglu_chunk():
        n = n_ref[...]
        a = jnp.dot(n, w1_ref[...], preferred_element_type=F32)
        b = jnp.dot(n, w3_ref[...], preferred_element_type=F32)
        h = (a * jax.nn.sigmoid(a)) * b
        return FFN_HALF * jnp.dot(h.astype(w2_ref.dtype), w2_ref[...], preferred_element_type=F32)

    @pl.when(k == 0)
    def _():
        n_ref[...] = _rms_norm(x_ref[...], g_ref[...]).astype(n_ref.dtype)
        o_ref[...] = x_ref[...] + half_swiglu_chunk()

    @pl.when(k > 0)
    def _():
        o_ref[...] += half_swiglu_chunk()

    if final_norm:
        @pl.when(k == pl.num_programs(1) - 1)
        def _():
            o_ref[...] = _rms_norm(o_ref[...], fg_ref[...])


def _ffn(x, g, w1, w3, w2, layer, final_g, *, final_norm):
    n_tok, d = x.shape
    d_ff = w1.shape[2]
    tm = _tile(n_tok, 1024)
    tf = _tile(d_ff, 512)
    blocks = (2 * _nbytes((tm, d), F32) + 2 * _nbytes((d, tf), MXU_DTYPE) + _nbytes((tf, d), MXU_DTYPE)
              + 2 * _nbytes((1, d), F32))
    scratch = _nbytes((tm, d), MXU_DTYPE)
    temps = 6 * _nbytes((tm, tf), F32)
    return pl.pallas_call(
        functools.partial(_ffn_body, final_norm=final_norm),
        out_shape=jax.ShapeDtypeStruct((n_tok, d), F32),
        grid=(n_tok // tm, d_ff // tf),
        in_specs=[
            pl.BlockSpec((tm, d), lambda i, k: (i, 0)),
            pl.BlockSpec((1, d), lambda i, k: (0, 0)),
            pl.BlockSpec((None, d, tf), lambda i, k: (layer, 0, k)),
            pl.BlockSpec((None, d, tf), lambda i, k: (layer, 0, k)),
            pl.BlockSpec((None, tf, d), lambda i, k: (layer, k, 0)),
            pl.BlockSpec((1, d), lambda i, k: (0, 0)),
        ],
        out_specs=pl.BlockSpec((tm, d), lambda i, k: (i, 0)),
        scratch_shapes=[pltpu.VMEM((tm, d), MXU_DTYPE)],
        compiler_params=pltpu.CompilerParams(
            dimension_semantics=("parallel", "arbitrary"),
            vmem_limit_bytes=_vmem_limit(blocks, scratch, temps)),
        name="ffn_final" if final_norm else "ffn",
    )(x, g.reshape(1, d), w1, w3, w2, final_g.reshape(1, d))


def _norm_matmul_body(x_ref, g_ref, w_ref, o_ref, n_ref):
    @pl.when(pl.program_id(1) == 0)
    def _():
        n_ref[...] = _rms_norm(x_ref[...], g_ref[...]).astype(n_ref.dtype)

    o_ref[...] = jnp.dot(n_ref[...], w_ref[...], preferred_element_type=F32)


def _norm_matmul_tail_body(x_ref, g_ref, w_ref, wt_ref, o_ref, ot_ref, n_ref):
    @pl.when(pl.program_id(1) == 0)
    def _():
        n_ref[...] = _rms_norm(x_ref[...], g_ref[...]).astype(n_ref.dtype)
        ot_ref[...] = jnp.dot(n_ref[...], wt_ref[...], preferred_element_type=F32)

    o_ref[...] = jnp.dot(n_ref[...], w_ref[...], preferred_element_type=F32)


def _norm_matmul(x, g, w, layer, n_out, *, w_tail=None, name):
    n_tok, d = x.shape
    tm = _tile(n_tok, 1024)
    tn = _tile(n_out, 1536)
    blocks = _nbytes((tm, d), F32) + _nbytes((1, d), F32) + _nbytes((d, tn), MXU_DTYPE) + _nbytes((tm, tn), F32)
    scratch = _nbytes((tm, d), MXU_DTYPE)
    temps = _nbytes((tm, tn), F32)
    in_specs = [
        pl.BlockSpec((tm, d), lambda i, j: (i, 0)),
        pl.BlockSpec((1, d), lambda i, j: (0, 0)),
        pl.BlockSpec((None, d, tn), lambda i, j: (layer, 0, j)),
    ]
    out_shape = jax.ShapeDtypeStruct((n_tok, n_out), F32)
    out_specs = pl.BlockSpec((tm, tn), lambda i, j: (i, j))
    args = (x, g.reshape(1, d), w)
    body = _norm_matmul_body
    if w_tail is not None:
        n_tail = w_tail.shape[1]
        blocks += _nbytes((d, n_tail), MXU_DTYPE) + _nbytes((tm, n_tail), F32)
        in_specs.append(pl.BlockSpec((d, n_tail), lambda i, j: (0, 0)))
        out_shape = (out_shape, jax.ShapeDtypeStruct((n_tok, n_tail), F32))
        out_specs = (out_specs, pl.BlockSpec((tm, n_tail), lambda i, j: (i, 0)))
        args = args + (w_tail,)
        body = _norm_matmul_tail_body
    return pl.pallas_call(
        body,
        out_shape=out_shape,
        grid=(n_tok // tm, n_out // tn),
        in_specs=in_specs,
        out_specs=out_specs,
        scratch_shapes=[pltpu.VMEM((tm, d), MXU_DTYPE)],
        compiler_params=pltpu.CompilerParams(
            dimension_semantics=("parallel", "arbitrary"),
            vmem_limit_bytes=_vmem_limit(blocks, scratch, temps)),
        name=name,
    )(*args)


def _matmul_residual_body(x_ref, y_ref, w_ref, o_ref):
    o_ref[...] = x_ref[...] + jnp.dot(y_ref[...], w_ref[...], preferred_element_type=F32)


def _matmul_residual(x, y, w, layer, *, name):
    n_tok, d = x.shape
    d_in = y.shape[1]
    tm = _tile(n_tok, 1024)
    tn = _tile(d, 1024)
    blocks = (2 * _nbytes((tm, tn), F32) + _nbytes((tm, d_in), y.dtype) + _nbytes((d_in, tn), MXU_DTYPE))
    temps = _nbytes((tm, tn), F32)
    return pl.pallas_call(
        _matmul_residual_body,
        out_shape=jax.ShapeDtypeStruct((n_tok, d), F32),
        grid=(n_tok // tm, d // tn),
        in_specs=[
            pl.BlockSpec((tm, tn), lambda i, j: (i, j)),
            pl.BlockSpec((tm, d_in), lambda i, j: (i, 0)),
            pl.BlockSpec((None, d_in, tn), lambda i, j: (layer, 0, j)),
        ],
        out_specs=pl.BlockSpec((tm, tn), lambda i, j: (i, j)),
        compiler_params=pltpu.CompilerParams(
            dimension_semantics=("parallel", "arbitrary"),
            vmem_limit_bytes=_vmem_limit(blocks, 0, temps)),
        name=name,
    )(x, y, w)


def _alibi_slopes(n_heads):
    ex = np.arange(1, n_heads + 1, dtype=np.float32) * np.float32(8.0 / n_heads)
    return np.power(np.float32(2.0), -ex)


def _even_core_body(sinks_ref, ub_ref, uc_ref, ux_ref, q_ref, kv_ref, kvp_ref, ucp_ref, uxp_ref, cw_ref,
                    x_ref, w_ref, o_ref, hs_ref, ks_ref, vts_ref, bias_ref, y_ref, *, slopes, tiles_per_seq):
    tq = ub_ref.shape[1]
    d_conv = ub_ref.shape[2]
    halo = ucp_ref.shape[1]
    blk = ATTN_BLOCK
    step = pl.program_id(0)
    tile = jnp.minimum(step, pl.num_programs(0) - 2)
    first_tile = lax.rem(tile, tiles_per_seq) == 0
    n_heads = bias_ref.shape[0]
    slot = lax.rem(step, 2)
    y_cur = y_ref.at[slot]
    y_prev = y_ref.at[1 - slot]

    @pl.when(step == 0)
    def _():
        y_ref[...] = jnp.zeros_like(y_ref)

    key = lax.broadcasted_iota(jnp.int32, (2 * blk, blk), 0)
    qry = lax.broadcasted_iota(jnp.int32, (2 * blk, blk), 1)

    @pl.when(step == 0)
    def _():
        dist_i = qry + blk - key
        dist = dist_i.astype(F32)
        valid = (dist_i >= 0) & (dist_i < WINDOW)
        for head in range(n_heads):
            bias_ref[head] = jnp.where(valid, -(float(slopes[head]) * dist), -jnp.inf)

    def short_conv():
        prev = ucp_ref[0] * uxp_ref[0]
        hs_ref[0:halo, :] = jnp.where(first_tile, jnp.zeros_like(prev), prev)
        hs_ref[halo:, :] = uc_ref[0] * ux_ref[0]
        y = None
        for j in range(CONV_WIDTH):
            off = halo - (CONV_WIDTH - 1) + j
            term = cw_ref[j:j + 1, :] * hs_ref[off:off + tq, :]
            y = term if y is None else y + term
        y_cur[:, 0:d_conv] = (ub_ref[0] * y).astype(y_cur.dtype)

    def project(c):
        cols = slice(c * V7X_MXU_DIM, (c + 1) * V7X_MXU_DIM)
        o_ref[:, cols] = x_ref[:, cols] + jnp.dot(y_prev[...], w_ref[:, cols], preferred_element_type=F32)

    n_slices = o_ref.shape[1] // V7X_MXU_DIM
    for c in range(PROJECT_BEFORE_STAGING):
        project(c)

    half = V7X_LANES // 2
    low = lax.broadcasted_iota(jnp.int32, (tq + blk, V7X_LANES), 1) < half
    top = lax.broadcasted_iota(jnp.int32, (V7X_LANES, tq + blk), 0) < half
    kv_all = jnp.concatenate([kvp_ref[0], kv_ref[0]], axis=0)
    n_slabs = kv_all.shape[1] // (2 * V7X_LANES)
    for s in range(n_slabs):
        slab = kv_all[:, s * V7X_LANES:(s + 1) * V7X_LANES]
        swapped = pltpu.roll(slab, half, axis=1)
        zero = jnp.zeros_like(slab)
        ks_ref[2 * s, 0] = jnp.where(low, slab, zero).astype(ks_ref.dtype)
        ks_ref[2 * s, 1] = jnp.where(low, zero, swapped).astype(ks_ref.dtype)
        ks_ref[2 * s + 1, 0] = jnp.where(low, swapped, zero).astype(ks_ref.dtype)
        ks_ref[2 * s + 1, 1] = jnp.where(low, zero, slab).astype(ks_ref.dtype)
        slab_t = kv_all[:, (n_slabs + s) * V7X_LANES:(n_slabs + s + 1) * V7X_LANES].T
        swapped_t = pltpu.roll(slab_t, half, axis=0)
        zero_t = jnp.zeros_like(slab_t)
        vts_ref[2 * s, 0] = jnp.where(top, slab_t, zero_t).astype(vts_ref.dtype)
        vts_ref[2 * s, 1] = jnp.where(top, zero_t, swapped_t).astype(vts_ref.dtype)
        vts_ref[2 * s + 1, 0] = jnp.where(top, swapped_t, zero_t).astype(vts_ref.dtype)
        vts_ref[2 * s + 1, 1] = jnp.where(top, zero_t, slab_t).astype(vts_ref.dtype)

    no_history = first_tile & (key < blk)
    scale = HEAD_DIM ** -0.5
    group = n_heads // N_KV_HEADS

    def scores(qb, g):
        rows = slice(qb * blk, (qb + 1) * blk)
        keys = slice(qb * blk, (qb + 2) * blk)
        slab0 = g * group // 2
        q_slabs = [q_ref[0, rows, (slab0 + a) * V7X_LANES:(slab0 + a + 1) * V7X_LANES]
                   for a in range(group // 2)]
        q4 = (jnp.concatenate(q_slabs, axis=0) * scale).astype(MXU_DTYPE)
        k_cat = jnp.concatenate([ks_ref[g, 0, keys, :], ks_ref[g, 1, keys, :]], axis=0)
        return _dot_nt(k_cat, q4)

    def attend(qb, g, st):
        rows = slice(qb * blk, (qb + 1) * blk)
        keys = slice(qb * blk, (qb + 2) * blk)
        slab0 = g * group // 2
        p_cols = []
        for a in range(group // 2):
            p_rows = []
            for hi in range(2):
                head = g * group + 2 * a + hi
                sq = st[hi * 2 * blk:(hi + 1) * 2 * blk, a * blk:(a + 1) * blk] + bias_ref[head]
                if qb == 0:
                    sq = jnp.where(no_history, -jnp.inf, sq)
                sink = sinks_ref[head]
                m = jnp.maximum(jnp.max(sq, axis=0, keepdims=True), sink)
                e = jnp.exp(sq - m)
                denom = jnp.sum(e, axis=0, keepdims=True) + jnp.exp(sink - m)
                p_rows.append((e * (1.0 / denom)).astype(MXU_DTYPE))
            p_cols.append(jnp.concatenate(p_rows, axis=0))
        pt = jnp.concatenate(p_cols, axis=1)
        vt_cat = jnp.concatenate([vts_ref[g, 0, :, keys], vts_ref[g, 1, :, keys]], axis=1)
        out = jnp.dot(vt_cat, pt, preferred_element_type=F32).T
        for a in range(group // 2):
            c0 = d_conv + (slab0 + a) * V7X_LANES
            y_cur[rows, c0:c0 + V7X_LANES] = out[a * blk:(a + 1) * blk].astype(y_cur.dtype)

    work = [(qb, g) for qb in range(tq // blk) for g in range(N_KV_HEADS)]
    pending = [scores(*w) for w in work[:ATTN_LOOKAHEAD]]
    for c in range(PROJECT_BEFORE_STAGING, n_slices):
        project(c)
    short_conv()
    for i, w in enumerate(work):
        if i + ATTN_LOOKAHEAD < len(work):
            pending.append(scores(*work[i + ATTN_LOOKAHEAD]))
        attend(*w, pending.pop(0))


def _even_core(x, u, conv_w, sinks, w_out, layer):
    bsz, seq, _ = u.shape
    n_tok, d = x.shape
    d_conv = conv_w.shape[1]
    n_q_heads = sinks.shape[0]
    d_attn = n_q_heads * HEAD_DIM
    d_kv = N_KV_HEADS * HEAD_DIM
    d_mix = d_conv + d_attn
    assert d_conv == d_attn and (n_q_heads // N_KV_HEADS) % 2 == 0 and (2 * d_kv) % V7X_LANES == 0
    tq = _tile(seq, 512)
    assert tq % ATTN_BLOCK == 0 and d % V7X_MXU_DIM == 0
    halo = V7X_SUBLANES
    blocks_per_tile = tq // ATTN_BLOCK
    halos_per_tile = tq // halo
    tiles_per_seq = seq // tq
    n_tiles = bsz * tiles_per_seq
    kv_col = (3 * d_conv + d_attn) // (2 * d_kv)
    slopes = _alibi_slopes(n_q_heads)

    def mixed(s):
        t = jnp.minimum(s, n_tiles - 1)
        return t // tiles_per_seq, t % tiles_per_seq

    def cur(col):
        return lambda s: (*mixed(s), col)

    def prev(per_tile, col):
        return lambda s: (mixed(s)[0], jnp.maximum(mixed(s)[1] * per_tile - 1, 0), col)

    def projected(s):
        return jnp.maximum(s - 1, 0), 0

    blocks = (4 * _nbytes((tq, d_conv), F32) + _nbytes((tq + ATTN_BLOCK, 2 * d_kv), F32)
              + 2 * _nbytes((halo, d_conv), F32) + 2 * _nbytes((tq, d), F32))
    scratch = (_nbytes((tq + halo, d_conv), F32) + 4 * _nbytes((tq + ATTN_BLOCK, 2 * d_kv), MXU_DTYPE)
               + _nbytes((n_q_heads, ATTN_BLOCK, 2 * ATTN_BLOCK), F32) + 2 * _nbytes((tq, d_mix), MXU_DTYPE)
               + _nbytes((d_mix, d), MXU_DTYPE))
    temps = 3 * _nbytes((tq, d_conv), F32) + 2 * _nbytes((tq + ATTN_BLOCK, 2 * d_kv), F32)
    return pl.pallas_call(
        functools.partial(_even_core_body, slopes=slopes, tiles_per_seq=tiles_per_seq),
        out_shape=jax.ShapeDtypeStruct((n_tok, d), F32),
        grid=(n_tiles + 1,),
        in_specs=[
            pl.BlockSpec(memory_space=pltpu.SMEM),
            pl.BlockSpec((1, tq, d_conv), cur(0)),
            pl.BlockSpec((1, tq, d_conv), cur(1)),
            pl.BlockSpec((1, tq, d_conv), cur(2)),
            pl.BlockSpec((1, tq, d_attn), cur(3)),
            pl.BlockSpec((1, tq, 2 * d_kv), cur(kv_col)),
            pl.BlockSpec((1, ATTN_BLOCK, 2 * d_kv), prev(blocks_per_tile, kv_col)),
            pl.BlockSpec((1, halo, d_conv), prev(halos_per_tile, 1)),
            pl.BlockSpec((1, halo, d_conv), prev(halos_per_tile, 2)),
            pl.BlockSpec((CONV_WIDTH, d_conv), lambda s: (0, 0)),
            pl.BlockSpec((tq, d), projected),
            pl.BlockSpec((None, d_mix, d), lambda s: (layer, 0, 0), pipeline_mode=pl.Buffered(1)),
        ],
        out_specs=pl.BlockSpec((tq, d), projected),
        scratch_shapes=[
            pltpu.VMEM((tq + halo, d_conv), F32),
            pltpu.VMEM((N_KV_HEADS, 2, tq + ATTN_BLOCK, V7X_LANES), MXU_DTYPE),
            pltpu.VMEM((N_KV_HEADS, 2, V7X_LANES, tq + ATTN_BLOCK), MXU_DTYPE),
            pltpu.VMEM((n_q_heads, 2 * ATTN_BLOCK, ATTN_BLOCK), F32),
            pltpu.VMEM((2, tq, d_mix), MXU_DTYPE),
        ],
        compiler_params=pltpu.CompilerParams(
            dimension_semantics=("arbitrary",),
            vmem_limit_bytes=_vmem_limit(blocks, scratch, temps)),
        name="even_mixer",
    )(sinks, u, u, u, u, u, u, u, u, conv_w, x, w_out)


def _split3(x):
    hi = x.astype(MXU_DTYPE)
    r1 = x - hi.astype(F32)
    mid = r1.astype(MXU_DTYPE)
    lo = (r1 - mid.astype(F32)).astype(MXU_DTYPE)
    return hi, mid, lo


GLA_PROLOGUE_STAGES = 3


def _gla_head(load, store, g_low, keep, tri, s_ref):
    q, k, v, r, w_up, b_up, head_g = load()
    tt, hk = q.shape
    ch = GLA_CHUNK
    n_chunks = tt // ch
    sub = keep.shape[0]

    z = jnp.dot(g_low, w_up, preferred_element_type=F32) + b_up
    log_a = (jnp.minimum(z, 0.0) - jnp.log(1.0 + jnp.exp(-jnp.abs(z)))) * (1.0 / GLA_GATE_NORMALIZER)
    parts = jnp.concatenate(_split3(log_a), axis=1)
    yield

    b = []
    for a in range(0, tt, sub):
        c3 = jnp.dot(tri, parts[a:a + sub], preferred_element_type=F32)
        b.append(c3[:, 0:hk] + c3[:, hk:2 * hk] + c3[:, 2 * hk:3 * hk])
    b = jnp.concatenate(b, axis=0)
    b_last = jnp.concatenate(
        [jnp.broadcast_to(b[(c + 1) * ch - 1:(c + 1) * ch, :], (ch, hk)) for c in range(n_chunks)], axis=0)

    q_dec = ((q * (hk ** -0.5)) * jnp.exp(b)).astype(MXU_DTYPE)
    k_inv = (k * jnp.exp(-b)).astype(MXU_DTYPE)
    k_tail_t = (k * jnp.exp(b_last - b)).T.astype(MXU_DTYPE)
    decay_t = jnp.exp(b_last.T)
    yield

    o_intra = []
    for a in range(0, tt, sub):
        attn = _dot_nt(q_dec[a:a + sub], k_inv[a:a + sub])
        attn = jnp.where(keep, attn, 0.0).astype(MXU_DTYPE)
        o_intra.append(jnp.dot(attn, v[a:a + sub], preferred_element_type=F32))
    o_intra = jnp.concatenate(o_intra, axis=0)
    yield

    o_inter = []
    for c in range(n_chunks):
        rows = slice(c * ch, (c + 1) * ch)
        state = s_ref[...]
        o_inter.append(jnp.dot(q_dec[rows], state.astype(MXU_DTYPE), preferred_element_type=F32))
        kv = jnp.dot(k_tail_t[:, rows], v[rows], preferred_element_type=F32)
        s_ref[...] = decay_t[:, c * ch:c * ch + 1] * state + kv
        yield
    o = o_intra + jnp.concatenate(o_inter, axis=0)

    o = _rms_norm(o, head_g)
    store(o * (r * jax.nn.sigmoid(r)))


def _gla_body(q_ref, k_ref, v_ref, r_ref, gl_ref, wgu_ref, bgu_ref, hg_ref, o_ref, s_ref):
    tt = q_ref.shape[1]
    n_heads = s_ref.shape[0]
    hk = q_ref.shape[2] // n_heads
    hv = v_ref.shape[2] // n_heads
    ch = GLA_CHUNK
    sub = min(tt, V7X_MXU_DIM)

    @pl.when(pl.program_id(1) == 0)
    def _():
        s_ref[...] = jnp.zeros_like(s_ref)

    ri = lax.broadcasted_iota(jnp.int32, (sub, sub), 0)
    ci = lax.broadcasted_iota(jnp.int32, (sub, sub), 1)
    keep = (ri // ch == ci // ch) & (ci <= ri)
    tri = jnp.where(keep, 1.0, 0.0).astype(MXU_DTYPE)

    g_low = gl_ref[0].astype(MXU_DTYPE)

    def head(h):
        ks = slice(h * hk, (h + 1) * hk)
        vs = slice(h * hv, (h + 1) * hv)

        def load():
            return (q_ref[0, :, ks], k_ref[0, :, ks], v_ref[0, :, vs].astype(MXU_DTYPE), r_ref[0, :, vs],
                    wgu_ref[:, ks], bgu_ref[:, ks], hg_ref[...])

        def store(o):
            o_ref[0, :, vs] = o.astype(o_ref.dtype)

        return _gla_head(load, store, g_low, keep, tri, s_ref.at[h])

    heads = [head(h) for h in range(n_heads)]
    finished = object()
    for _ in range(GLA_PROLOGUE_STAGES):
        next(heads[0])
    for h in range(n_heads):
        ahead = heads[h + 1] if h + 1 < n_heads else None
        todo = GLA_PROLOGUE_STAGES if ahead is not None else 0
        for _ in range(tt // ch):
            next(heads[h])
            if todo:
                next(ahead)
                todo -= 1
        assert next(heads[h], finished) is finished
        for _ in range(todo):
            next(ahead)


def _gla_core(u, g_low, w_gate_up, b_gate_up, head_g):
    bsz, seq, _ = u.shape
    gl_w, d_k = w_gate_up.shape
    hk = d_k // GLA_HEADS
    hv = head_g.shape[0]
    d_v = hv * GLA_HEADS
    assert d_v % d_k == 0
    tt = _tile(seq, 512)
    assert GLA_CHUNK % V7X_LANES == 0 and tt % GLA_CHUNK == 0 and min(tt, V7X_MXU_DIM) % GLA_CHUNK == 0
    blocks = (2 * _nbytes((tt, d_k), F32) + 2 * _nbytes((tt, d_v), F32) + _nbytes((tt, gl_w), F32)
              + _nbytes((gl_w, d_k), MXU_DTYPE) + _nbytes((1, d_k), F32) + _nbytes((1, hv), F32)
              + _nbytes((tt, d_v), MXU_DTYPE))
    scratch = _nbytes((GLA_HEADS, hk, hv), F32)
    temps = GLA_HEADS * (8 * _nbytes((tt, hk), F32) + 4 * _nbytes((tt, hv), F32))
    return pl.pallas_call(
        _gla_body,
        out_shape=jax.ShapeDtypeStruct((bsz, seq, d_v), MXU_DTYPE),
        grid=(bsz, seq // tt),
        in_specs=[
            pl.BlockSpec((1, tt, d_k), lambda b, t: (b, t, 0)),
            pl.BlockSpec((1, tt, d_k), lambda b, t: (b, t, 1)),
            pl.BlockSpec((1, tt, d_v), lambda b, t: (b, t, 2 * d_k // d_v)),
            pl.BlockSpec((1, tt, d_v), lambda b, t: (b, t, 2 * d_k // d_v + 1)),
            pl.BlockSpec((1, tt, gl_w), lambda b, t: (b, t, 0)),
            pl.BlockSpec((gl_w, d_k), lambda b, t: (0, 0)),
            pl.BlockSpec((1, d_k), lambda b, t: (0, 0)),
            pl.BlockSpec((1, hv), lambda b, t: (0, 0)),
        ],
        out_specs=pl.BlockSpec((1, tt, d_v), lambda b, t: (b, t, 0)),
        scratch_shapes=[pltpu.VMEM((GLA_HEADS, hk, hv), F32)],
        compiler_params=pltpu.CompilerParams(
            dimension_semantics=("parallel", "arbitrary"),
            vmem_limit_bytes=_vmem_limit(blocks, scratch, temps)),
        name="gla_core",
    )(u, u, u, u, g_low, w_gate_up, b_gate_up.reshape(1, d_k), head_g.reshape(1, hv))


def _even_mixer(x, g, w_in, conv_w, sinks, w_out, i, bsz, seq):
    u = _norm_matmul(x, g, w_in, i, w_in.shape[2], name="even_in_proj")
    return _even_core(x, u.reshape(bsz, seq, -1), conv_w, sinks, w_out, i)


def _odd_mixer(x, g, w_in, w_in_f32, w_gate_up, b_gate_up, head_g, w_out, i, bsz, seq):
    rank, d_k = w_gate_up.shape
    d_main = w_in.shape[2] - rank
    w_low = jnp.pad(w_in_f32[i, :, d_main:], ((0, 0), (0, V7X_LANES - rank))).astype(MXU_DTYPE)
    w_up = jnp.pad(w_gate_up, ((0, V7X_LANES - rank), (0, 0))).astype(MXU_DTYPE)
    u, g_low = _norm_matmul(x, g, w_in, i, d_main, w_tail=w_low, name="odd_in_proj")
    o = _gla_core(u.reshape(bsz, seq, -1), g_low.reshape(bsz, seq, -1), w_up, b_gate_up, head_g)
    return _matmul_residual(x, o.reshape(bsz * seq, -1), w_out, i, name="odd_out_proj")


def kernel(x, norm_g, ffn_pre_w1, ffn_pre_w3, ffn_pre_w2, ffn_post_w1, ffn_post_w3, ffn_post_w2, even_w_in,
           even_conv_w, even_sinks, even_w_out, odd_w_in, odd_w_gate_up, odd_b_gate_up, odd_head_g, odd_w_out,
           final_g):
    bsz, seq, d = x.shape
    depth = norm_g.shape[0]
    h = x.reshape(bsz * seq, d)
    pre = [w.astype(MXU_DTYPE) for w in (ffn_pre_w1, ffn_pre_w3, ffn_pre_w2)]
    post = [w.astype(MXU_DTYPE) for w in (ffn_post_w1, ffn_post_w3, ffn_post_w2)]
    even_in, even_out = even_w_in.astype(MXU_DTYPE), even_w_out.astype(MXU_DTYPE)
    odd_in, odd_out = odd_w_in.astype(MXU_DTYPE), odd_w_out.astype(MXU_DTYPE)
    for layer in range(depth):
        h = _ffn(h, norm_g[layer, 0], *pre, layer, final_g, final_norm=False)
        i = layer // 2
        if layer % 2 == 0:
            h = _even_mixer(h, norm_g[layer, 1], even_in, even_conv_w[i], even_sinks[i], even_out, i, bsz, seq)
        else:
            h = _odd_mixer(h, norm_g[layer, 1], odd_in, odd_w_in, odd_w_gate_up[i], odd_b_gate_up[i],
                           odd_head_g[i], odd_out, i, bsz, seq)
        h = _ffn(h, norm_g[layer, 2], *post, layer, final_g, final_norm=(layer == depth - 1))
    return h.reshape(bsz, seq, d)
```

```python
import functools

import jax
import jax.numpy as jnp
import numpy as np
from jax import lax
from jax.experimental import pallas as pl
from jax.experimental.pallas import tpu as pltpu

F32 = jnp.float32
MXU_DTYPE = jnp.bfloat16

NORM_EPS = 1e-6
FFN_HALF = 0.5
CONV_WIDTH = 3
HEAD_DIM = 64
N_KV_HEADS = 4
WINDOW = 128
ATTN_BLOCK = 128
GLA_HEADS = 4
GLA_CHUNK = 128
GLA_GATE_NORMALIZER = 16.0
ATTN_LOOKAHEAD = 6
PROJECT_BEFORE_STAGING = 2

V7X_LANES = 128
V7X_SUBLANES = 8
BF16_SUBLANES = 16
V7X_MXU_DIM = 256
V7X_VMEM_BYTES = 64 * 1024 * 1024


def _vmem_limit(block_bytes, scratch_bytes, temp_bytes):
    need = 2 * block_bytes + scratch_bytes + temp_bytes
    return int(min(need, V7X_VMEM_BYTES))


def _nbytes(shape, dtype):
    return int(np.prod(shape)) * jnp.dtype(dtype).itemsize


def _tile(n, want):
    if n <= want:
        return n
    t = want
    while n % t:
        t -= V7X_SUBLANES
    return t


def _rms_norm(x, g):
    return x * lax.rsqrt(jnp.mean(x * x, axis=-1, keepdims=True) + NORM_EPS) * g


def _dot_nt(a, b):
    return lax.dot_general(a, b, (((1,), (1,)), ((), ())), preferred_element_type=F32)


def _ffn_body(x_ref, g_ref, w1_ref, w3_ref, w2_ref, fg_ref, *rest, final_norm, n_cast):
    cast_src, o_ref, cast_dst, n_ref = rest[:n_cast], rest[n_cast], rest[n_cast + 1:2 * n_cast + 1], rest[-1]
    k = pl.program_id(1)

    def half_swiglu_chunk():
        for src, dst in zip(cast_src, cast_dst):
            dst[...] = src[...].astype(dst.dtype)
        n = n_ref[...]
        a = jnp.dot(n, w1_ref[...], preferred_element_type=F32)
        b = jnp.dot(n, w3_ref[...], preferred_element_type=F32)
        h = (a * jax.nn.sigmoid(a)) * b
        return FFN_HALF * jnp.dot(h.astype(w2_ref.dtype), w2_ref[...], preferred_element_type=F32)

    @pl.when(k == 0)
    def _():
        n_ref[...] = _rms_norm(x_ref[...], g_ref[...]).astype(n_ref.dtype)
        o_ref[...] = x_ref[...] + half_swiglu_chunk()

    @pl.when(k > 0)
    def _():
        o_ref[...] += half_swiglu_chunk()

    if final_norm:
        @pl.when(k == pl.num_programs(1) - 1)
        def _():
            o_ref[...] = _rms_norm(o_ref[...], fg_ref[...])


def _ffn(x, g, weights, layer, final_g, *, final_norm, cast_next=None):
    w1, w3, w2 = weights
    n_tok, d = x.shape
    d_ff = w1.shape[2]
    tm = _tile(n_tok, 1024)
    tf = _tile(d_ff, 512)
    gi, gk = n_tok // tm, d_ff // tf
    blocks = (2 * _nbytes((tm, d), F32) + 2 * _nbytes((d, tf), MXU_DTYPE) + _nbytes((tf, d), MXU_DTYPE)
              + 2 * _nbytes((1, d), F32))
    scratch = _nbytes((tm, d), MXU_DTYPE)
    temps = 6 * _nbytes((tm, tf), F32)
    in_specs = [
        pl.BlockSpec((tm, d), lambda i, k: (i, 0)),
        pl.BlockSpec((1, d), lambda i, k: (0, 0)),
        pl.BlockSpec((None, d, tf), lambda i, k: (layer, 0, k)),
        pl.BlockSpec((None, d, tf), lambda i, k: (layer, 0, k)),
        pl.BlockSpec((None, tf, d), lambda i, k: (layer, k, 0)),
        pl.BlockSpec((1, d), lambda i, k: (0, 0)),
    ]
    args = [x, g.reshape(1, d), w1, w3, w2, final_g.reshape(1, d)]
    out_shape = [jax.ShapeDtypeStruct((n_tok, d), F32)]
    out_specs = [pl.BlockSpec((tm, d), lambda i, k: (i, 0))]
    n_cast = 0
    if cast_next is not None:
        (f1, f3, f2), nxt = cast_next
        rows_in, rows_out = d // gi, d_ff // (gi * gk)
        assert d % gi == 0 and d_ff % (gi * gk) == 0
        assert rows_in % BF16_SUBLANES == 0 and rows_out % BF16_SUBLANES == 0
        n_cast = 3
        args += [f1, f3, f2]
        in_specs += [
            pl.BlockSpec((None, rows_in, tf), lambda i, k: (nxt, i, k)),
            pl.BlockSpec((None, rows_in, tf), lambda i, k: (nxt, i, k)),
            pl.BlockSpec((None, rows_out, d), lambda i, k: (nxt, i * gk + k, 0)),
        ]
        out_shape += [jax.ShapeDtypeStruct((1, d, d_ff), MXU_DTYPE), jax.ShapeDtypeStruct((1, d, d_ff), MXU_DTYPE),
                      jax.ShapeDtypeStruct((1, d_ff, d), MXU_DTYPE)]
        out_specs += [
            pl.BlockSpec((None, rows_in, tf), lambda i, k: (0, i, k)),
            pl.BlockSpec((None, rows_in, tf), lambda i, k: (0, i, k)),
            pl.BlockSpec((None, rows_out, d), lambda i, k: (0, i * gk + k, 0)),
        ]
        blocks += 2 * (_nbytes((rows_in, tf), F32) + _nbytes((rows_in, tf), MXU_DTYPE))
        blocks += _nbytes((rows_out, d), F32) + _nbytes((rows_out, d), MXU_DTYPE)
    out = pl.pallas_call(
        functools.partial(_ffn_body, final_norm=final_norm, n_cast=n_cast),
        out_shape=out_shape,
        grid=(gi, gk),
        in_specs=in_specs,
        out_specs=out_specs,
        scratch_shapes=[pltpu.VMEM((tm, d), MXU_DTYPE)],
        compiler_params=pltpu.CompilerParams(
            dimension_semantics=("parallel", "arbitrary"),
            vmem_limit_bytes=_vmem_limit(blocks, scratch, temps)),
        name="ffn_final" if final_norm else "ffn",
    )(*args)
    return out[0], tuple(out[1:])


def _norm_matmul_body(x_ref, g_ref, w_ref, o_ref, n_ref):
    @pl.when(pl.program_id(1) == 0)
    def _():
        n_ref[...] = _rms_norm(x_ref[...], g_ref[...]).astype(n_ref.dtype)

    o_ref[...] = jnp.dot(n_ref[...], w_ref[...], preferred_element_type=F32)


def _norm_matmul_tail_body(x_ref, g_ref, w_ref, wt_ref, o_ref, ot_ref, n_ref):
    @pl.when(pl.program_id(1) == 0)
    def _():
        n_ref[...] = _rms_norm(x_ref[...], g_ref[...]).astype(n_ref.dtype)
        ot_ref[...] = jnp.dot(n_ref[...], wt_ref[...], preferred_element_type=F32)

    o_ref[...] = jnp.dot(n_ref[...], w_ref[...], preferred_element_type=F32)


def _norm_matmul(x, g, w, layer, n_out, *, w_tail=None, name):
    n_tok, d = x.shape
    tm = _tile(n_tok, 1024)
    tn = _tile(n_out, 1536)
    blocks = _nbytes((tm, d), F32) + _nbytes((1, d), F32) + _nbytes((d, tn), MXU_DTYPE) + _nbytes((tm, tn), F32)
    scratch = _nbytes((tm, d), MXU_DTYPE)
    temps = _nbytes((tm, tn), F32)
    in_specs = [
        pl.BlockSpec((tm, d), lambda i, j: (i, 0)),
        pl.BlockSpec((1, d), lambda i, j: (0, 0)),
        pl.BlockSpec((None, d, tn), lambda i, j: (layer, 0, j)),
    ]
    out_shape = jax.ShapeDtypeStruct((n_tok, n_out), F32)
    out_specs = pl.BlockSpec((tm, tn), lambda i, j: (i, j))
    args = (x, g.reshape(1, d), w)
    body = _norm_matmul_body
    if w_tail is not None:
        n_tail = w_tail.shape[1]
        blocks += _nbytes((d, n_tail), MXU_DTYPE) + _nbytes((tm, n_tail), F32)
        in_specs.append(pl.BlockSpec((d, n_tail), lambda i, j: (0, 0)))
        out_shape = (out_shape, jax.ShapeDtypeStruct((n_tok, n_tail), F32))
        out_specs = (out_specs, pl.BlockSpec((tm, n_tail), lambda i, j: (i, 0)))
        args = args + (w_tail,)
        body = _norm_matmul_tail_body
    return pl.pallas_call(
        body,
        out_shape=out_shape,
        grid=(n_tok // tm, n_out // tn),
        in_specs=in_specs,
        out_specs=out_specs,
        scratch_shapes=[pltpu.VMEM((tm, d), MXU_DTYPE)],
        compiler_params=pltpu.CompilerParams(
            dimension_semantics=("parallel", "arbitrary"),
            vmem_limit_bytes=_vmem_limit(blocks, scratch, temps)),
        name=name,
    )(*args)


def _matmul_residual_body(x_ref, y_ref, w_ref, o_ref):
    o_ref[...] = x_ref[...] + jnp.dot(y_ref[...], w_ref[...], preferred_element_type=F32)


def _matmul_residual(x, y, w, layer, *, name):
    n_tok, d = x.shape
    d_in = y.shape[1]
    tm = _tile(n_tok, 1024)
    tn = _tile(d, 1024)
    blocks = (2 * _nbytes((tm, tn), F32) + _nbytes((tm, d_in), y.dtype) + _nbytes((d_in, tn), MXU_DTYPE))
    temps = _nbytes((tm, tn), F32)
    return pl.pallas_call(
        _matmul_residual_body,
        out_shape=jax.ShapeDtypeStruct((n_tok, d), F32),
        grid=(n_tok // tm, d // tn),
        in_specs=[
            pl.BlockSpec((tm, tn), lambda i, j: (i, j)),
            pl.BlockSpec((tm, d_in), lambda i, j: (i, 0)),
            pl.BlockSpec((None, d_in, tn), lambda i, j: (layer, 0, j)),
        ],
        out_specs=pl.BlockSpec((tm, tn), lambda i, j: (i, j)),
        compiler_params=pltpu.CompilerParams(
            dimension_semantics=("parallel", "arbitrary"),
            vmem_limit_bytes=_vmem_limit(blocks, 0, temps)),
        name=name,
    )(x, y, w)


def _alibi_slopes(n_heads):
    ex = np.arange(1, n_heads + 1, dtype=np.float32) * np.float32(8.0 / n_heads)
    return np.power(np.float32(2.0), -ex)


def _even_core_body(sinks_ref, ub_ref, uc_ref, ux_ref, q_ref, kv_ref, kvp_ref, ucp_ref, uxp_ref, cw_ref,
                    x_ref, w_ref, o_ref, hs_ref, ks_ref, vts_ref, bias_ref, y_ref, *, slopes, tiles_per_seq):
    tq = ub_ref.shape[1]
    d_conv = ub_ref.shape[2]
    halo = ucp_ref.shape[1]
    blk = ATTN_BLOCK
    step = pl.program_id(0)
    tile = jnp.minimum(step, pl.num_programs(0) - 2)
    first_tile = lax.rem(tile, tiles_per_seq) == 0
    n_heads = bias_ref.shape[0]
    slot = lax.rem(step, 2)
    y_cur = y_ref.at[slot]
    y_prev = y_ref.at[1 - slot]

    @pl.when(step == 0)
    def _():
        y_ref[...] = jnp.zeros_like(y_ref)

    key = lax.broadcasted_iota(jnp.int32, (2 * blk, blk), 0)
    qry = lax.broadcasted_iota(jnp.int32, (2 * blk, blk), 1)

    @pl.when(step == 0)
    def _():
        dist_i = qry + blk - key
        dist = dist_i.astype(F32)
        valid = (dist_i >= 0) & (dist_i < WINDOW)
        for head in range(n_heads):
            bias_ref[head] = jnp.where(valid, -(float(slopes[head]) * dist), -jnp.inf)

    def short_conv():
        prev = ucp_ref[0] * uxp_ref[0]
        hs_ref[0:halo, :] = jnp.where(first_tile, jnp.zeros_like(prev), prev)
        hs_ref[halo:, :] = uc_ref[0] * ux_ref[0]
        y = None
        for j in range(CONV_WIDTH):
            off = halo - (CONV_WIDTH - 1) + j
            term = cw_ref[j:j + 1, :] * hs_ref[off:off + tq, :]
            y = term if y is None else y + term
        y_cur[:, 0:d_conv] = (ub_ref[0] * y).astype(y_cur.dtype)

    def project(c):
        cols = slice(c * V7X_MXU_DIM, (c + 1) * V7X_MXU_DIM)
        o_ref[:, cols] = x_ref[:, cols] + jnp.dot(y_prev[...], w_ref[:, cols], preferred_element_type=F32)

    n_slices = o_ref.shape[1] // V7X_MXU_DIM
    for c in range(PROJECT_BEFORE_STAGING):
        project(c)

    half = V7X_LANES // 2
    low = lax.broadcasted_iota(jnp.int32, (tq + blk, V7X_LANES), 1) < half
    top = lax.broadcasted_iota(jnp.int32, (V7X_LANES, tq + blk), 0) < half
    kv_all = jnp.concatenate([kvp_ref[0], kv_ref[0]], axis=0)
    n_slabs = kv_all.shape[1] // (2 * V7X_LANES)
    for s in range(n_slabs):
        slab = kv_all[:, s * V7X_LANES:(s + 1) * V7X_LANES]
        swapped = pltpu.roll(slab, half, axis=1)
        zero = jnp.zeros_like(slab)
        ks_ref[2 * s, 0] = jnp.where(low, slab, zero).astype(ks_ref.dtype)
        ks_ref[2 * s, 1] = jnp.where(low, zero, swapped).astype(ks_ref.dtype)
        ks_ref[2 * s + 1, 0] = jnp.where(low, swapped, zero).astype(ks_ref.dtype)
        ks_ref[2 * s + 1, 1] = jnp.where(low, zero, slab).astype(ks_ref.dtype)
        slab_t = kv_all[:, (n_slabs + s) * V7X_LANES:(n_slabs + s + 1) * V7X_LANES].T
        swapped_t = pltpu.roll(slab_t, half, axis=0)
        zero_t = jnp.zeros_like(slab_t)
        vts_ref[2 * s, 0] = jnp.where(top, slab_t, zero_t).astype(vts_ref.dtype)
        vts_ref[2 * s, 1] = jnp.where(top, zero_t, swapped_t).astype(vts_ref.dtype)
        vts_ref[2 * s + 1, 0] = jnp.where(top, swapped_t, zero_t).astype(vts_ref.dtype)
        vts_ref[2 * s + 1, 1] = jnp.where(top, zero_t, slab_t).astype(vts_ref.dtype)

    no_history = first_tile & (key < blk)
    scale = HEAD_DIM ** -0.5
    group = n_heads // N_KV_HEADS

    def scores(qb, g):
        rows = slice(qb * blk, (qb + 1) * blk)
        keys = slice(qb * blk, (qb + 2) * blk)
        slab0 = g * group // 2
        q_slabs = [q_ref[0, rows, (slab0 + a) * V7X_LANES:(slab0 + a + 1) * V7X_LANES]
                   for a in range(group // 2)]
        q4 = (jnp.concatenate(q_slabs, axis=0) * scale).astype(MXU_DTYPE)
        k_cat = jnp.concatenate([ks_ref[g, 0, keys, :], ks_ref[g, 1, keys, :]], axis=0)
        return _dot_nt(k_cat, q4)

    def attend(qb, g, st):
        rows = slice(qb * blk, (qb + 1) * blk)
        keys = slice(qb * blk, (qb + 2) * blk)
        slab0 = g * group // 2
        p_cols = []
        for a in range(group // 2):
            p_rows = []
            for hi in range(2):
                head = g * group + 2 * a + hi
                sq = st[hi * 2 * blk:(hi + 1) * 2 * blk, a * blk:(a + 1) * blk] + bias_ref[head]
                if qb == 0:
                    sq = jnp.where(no_history, -jnp.inf, sq)
                sink = sinks_ref[head]
                m = jnp.maximum(jnp.max(sq, axis=0, keepdims=True), sink)
                e = jnp.exp(sq - m)
                denom = jnp.sum(e, axis=0, keepdims=True) + jnp.exp(sink - m)
                p_rows.append((e * (1.0 / denom)).astype(MXU_DTYPE))
            p_cols.append(jnp.concatenate(p_rows, axis=0))
        pt = jnp.concatenate(p_cols, axis=1)
        vt_cat = jnp.concatenate([vts_ref[g, 0, :, keys], vts_ref[g, 1, :, keys]], axis=1)
        out = jnp.dot(vt_cat, pt, preferred_element_type=F32).T
        for a in range(group // 2):
            c0 = d_conv + (slab0 + a) * V7X_LANES
            y_cur[rows, c0:c0 + V7X_LANES] = out[a * blk:(a + 1) * blk].astype(y_cur.dtype)

    work = [(qb, g) for qb in range(tq // blk) for g in range(N_KV_HEADS)]
    pending = [scores(*w) for w in work[:ATTN_LOOKAHEAD]]
    for c in range(PROJECT_BEFORE_STAGING, n_slices):
        project(c)
    short_conv()
    for i, w in enumerate(work):
        if i + ATTN_LOOKAHEAD < len(work):
            pending.append(scores(*work[i + ATTN_LOOKAHEAD]))
        attend(*w, pending.pop(0))


def _even_core(x, u, conv_w, sinks, w_out, layer):
    bsz, seq, _ = u.shape
    n_tok, d = x.shape
    d_conv = conv_w.shape[1]
    n_q_heads = sinks.shape[0]
    d_attn = n_q_heads * HEAD_DIM
    d_kv = N_KV_HEADS * HEAD_DIM
    d_mix = d_conv + d_attn
    assert d_conv == d_attn and (n_q_heads // N_KV_HEADS) % 2 == 0 and (2 * d_kv) % V7X_LANES == 0
    tq = _tile(seq, 512)
    assert tq % ATTN_BLOCK == 0 and d % V7X_MXU_DIM == 0
    halo = V7X_SUBLANES
    blocks_per_tile = tq // ATTN_BLOCK
    halos_per_tile = tq // halo
    tiles_per_seq = seq // tq
    n_tiles = bsz * tiles_per_seq
    kv_col = (3 * d_conv + d_attn) // (2 * d_kv)
    slopes = _alibi_slopes(n_q_heads)

    def mixed(s):
        t = jnp.minimum(s, n_tiles - 1)
        return t // tiles_per_seq, t % tiles_per_seq

    def cur(col):
        return lambda s: (*mixed(s), col)

    def prev(per_tile, col):
        return lambda s: (mixed(s)[0], jnp.maximum(mixed(s)[1] * per_tile - 1, 0), col)

    def projected(s):
        return jnp.maximum(s - 1, 0), 0

    blocks = (4 * _nbytes((tq, d_conv), F32) + _nbytes((tq + ATTN_BLOCK, 2 * d_kv), F32)
              + 2 * _nbytes((halo, d_conv), F32) + 2 * _nbytes((tq, d), F32))
    scratch = (_nbytes((tq + halo, d_conv), F32) + 4 * _nbytes((tq + ATTN_BLOCK, 2 * d_kv), MXU_DTYPE)
               + _nbytes((n_q_heads, ATTN_BLOCK, 2 * ATTN_BLOCK), F32) + 2 * _nbytes((tq, d_mix), MXU_DTYPE)
               + _nbytes((d_mix, d), MXU_DTYPE))
    temps = 3 * _nbytes((tq, d_conv), F32) + 2 * _nbytes((tq + ATTN_BLOCK, 2 * d_kv), F32)
    return pl.pallas_call(
        functools.partial(_even_core_body, slopes=slopes, tiles_per_seq=tiles_per_seq),
        out_shape=jax.ShapeDtypeStruct((n_tok, d), F32),
        grid=(n_tiles + 1,),
        in_specs=[
            pl.BlockSpec(memory_space=pltpu.SMEM),
            pl.BlockSpec((1, tq, d_conv), cur(0)),
            pl.BlockSpec((1, tq, d_conv), cur(1)),
            pl.BlockSpec((1, tq, d_conv), cur(2)),
            pl.BlockSpec((1, tq, d_attn), cur(3)),
            pl.BlockSpec((1, tq, 2 * d_kv), cur(kv_col)),
            pl.BlockSpec((1, ATTN_BLOCK, 2 * d_kv), prev(blocks_per_tile, kv_col)),
            pl.BlockSpec((1, halo, d_conv), prev(halos_per_tile, 1)),
            pl.BlockSpec((1, halo, d_conv), prev(halos_per_tile, 2)),
            pl.BlockSpec((CONV_WIDTH, d_conv), lambda s: (0, 0)),
            pl.BlockSpec((tq, d), projected),
            pl.BlockSpec((None, d_mix, d), lambda s: (layer, 0, 0), pipeline_mode=pl.Buffered(1)),
        ],
        out_specs=pl.BlockSpec((tq, d), projected),
        scratch_shapes=[
            pltpu.VMEM((tq + halo, d_conv), F32),
            pltpu.VMEM((N_KV_HEADS, 2, tq + ATTN_BLOCK, V7X_LANES), MXU_DTYPE),
            pltpu.VMEM((N_KV_HEADS, 2, V7X_LANES, tq + ATTN_BLOCK), MXU_DTYPE),
            pltpu.VMEM((n_q_heads, 2 * ATTN_BLOCK, ATTN_BLOCK), F32),
            pltpu.VMEM((2, tq, d_mix), MXU_DTYPE),
        ],
        compiler_params=pltpu.CompilerParams(
            dimension_semantics=("arbitrary",),
            vmem_limit_bytes=_vmem_limit(blocks, scratch, temps)),
        name="even_mixer",
    )(sinks, u, u, u, u, u, u, u, u, conv_w, x, w_out)


def _split3(x):
    hi = x.astype(MXU_DTYPE)
    r1 = x - hi.astype(F32)
    mid = r1.astype(MXU_DTYPE)
    lo = (r1 - mid.astype(F32)).astype(MXU_DTYPE)
    return hi, mid, lo


GLA_PROLOGUE_STAGES = 3


def _gla_head(load, store, g_low, keep, tri, s_ref):
    q, k, v, r, w_up, b_up, head_g = load()
    tt, hk = q.shape
    ch = GLA_CHUNK
    n_chunks = tt // ch
    sub = keep.shape[0]

    z = jnp.dot(g_low, w_up, preferred_element_type=F32) + b_up
    log_a = (jnp.minimum(z, 0.0) - jnp.log(1.0 + jnp.exp(-jnp.abs(z)))) * (1.0 / GLA_GATE_NORMALIZER)
    parts = jnp.concatenate(_split3(log_a), axis=1)
    yield

    b = []
    for a in range(0, tt, sub):
        c3 = jnp.dot(tri, parts[a:a + sub], preferred_element_type=F32)
        b.append(c3[:, 0:hk] + c3[:, hk:2 * hk] + c3[:, 2 * hk:3 * hk])
    b = jnp.concatenate(b, axis=0)
    decay = jnp.concatenate(
        [jnp.broadcast_to(jnp.exp(b[(c + 1) * ch - 1:(c + 1) * ch, :]), (ch, hk)) for c in range(n_chunks)],
        axis=0)
    grow = jnp.exp(b)
    q_dec = ((q * (hk ** -0.5)) * grow).astype(MXU_DTYPE)
    k_inv_f32 = k * (1.0 / grow)
    k_inv = k_inv_f32.astype(MXU_DTYPE)
    k_tail_t = (k_inv_f32 * decay).T.astype(MXU_DTYPE)
    decay_t = decay.T
    yield

    o_intra = []
    for a in range(0, tt, sub):
        attn = _dot_nt(q_dec[a:a + sub], k_inv[a:a + sub])
        attn = jnp.where(keep, attn, 0.0).astype(MXU_DTYPE)
        o_intra.append(jnp.dot(attn, v[a:a + sub], preferred_element_type=F32))
    o_intra = jnp.concatenate(o_intra, axis=0)
    yield

    o_inter = []
    for c in range(n_chunks):
        rows = slice(c * ch, (c + 1) * ch)
        state = s_ref[...]
        o_inter.append(jnp.dot(q_dec[rows], state.astype(MXU_DTYPE), preferred_element_type=F32))
        kv = jnp.dot(k_tail_t[:, rows], v[rows], preferred_element_type=F32)
        s_ref[...] = decay_t[:, c * ch:c * ch + 1] * state + kv
        yield
    o = o_intra + jnp.concatenate(o_inter, axis=0)

    o = _rms_norm(o, head_g)
    half_r = 0.5 * r
    store(o * (half_r + half_r * jnp.tanh(half_r)))


def _gla_body(q_ref, k_ref, v_ref, r_ref, gl_ref, wgu_ref, bgu_ref, hg_ref, o_ref, s_ref):
    tt = q_ref.shape[1]
    n_heads = s_ref.shape[0]
    hk = q_ref.shape[2] // n_heads
    hv = v_ref.shape[2] // n_heads
    ch = GLA_CHUNK
    sub = min(tt, V7X_MXU_DIM)

    @pl.when(pl.program_id(1) == 0)
    def _():
        s_ref[...] = jnp.zeros_like(s_ref)

    ri = lax.broadcasted_iota(jnp.int32, (sub, sub), 0)
    ci = lax.broadcasted_iota(jnp.int32, (sub, sub), 1)
    keep = (ri // ch == ci // ch) & (ci <= ri)
    tri = jnp.where(keep, 1.0, 0.0).astype(MXU_DTYPE)

    g_low = gl_ref[0].astype(MXU_DTYPE)

    def head(h):
        ks = slice(h * hk, (h + 1) * hk)
        vs = slice(h * hv, (h + 1) * hv)

        def load():
            return (q_ref[0, :, ks], k_ref[0, :, ks], v_ref[0, :, vs].astype(MXU_DTYPE), r_ref[0, :, vs],
                    wgu_ref[:, ks], bgu_ref[:, ks], hg_ref[...])

        def store(o):
            o_ref[0, :, vs] = o.astype(o_ref.dtype)

        return _gla_head(load, store, g_low, keep, tri, s_ref.at[h])

    heads = [head(h) for h in range(n_heads)]
    finished = object()
    for _ in range(GLA_PROLOGUE_STAGES):
        next(heads[0])
    for h in range(n_heads):
        ahead = heads[h + 1] if h + 1 < n_heads else None
        todo = GLA_PROLOGUE_STAGES if ahead is not None else 0
        for _ in range(tt // ch):
            next(heads[h])
            if todo:
                next(ahead)
                todo -= 1
        assert next(heads[h], finished) is finished
        for _ in range(todo):
            next(ahead)


def _gla_core(u, g_low, w_gate_up, b_gate_up, head_g):
    bsz, seq, _ = u.shape
    gl_w, d_k = w_gate_up.shape
    hk = d_k // GLA_HEADS
    hv = head_g.shape[0]
    d_v = hv * GLA_HEADS
    assert d_v % d_k == 0
    tt = _tile(seq, 512)
    assert GLA_CHUNK % V7X_LANES == 0 and tt % GLA_CHUNK == 0 and min(tt, V7X_MXU_DIM) % GLA_CHUNK == 0
    blocks = (2 * _nbytes((tt, d_k), F32) + 2 * _nbytes((tt, d_v), F32) + _nbytes((tt, gl_w), F32)
              + _nbytes((gl_w, d_k), MXU_DTYPE) + _nbytes((1, d_k), F32) + _nbytes((1, hv), F32)
              + _nbytes((tt, d_v), MXU_DTYPE))
    scratch = _nbytes((GLA_HEADS, hk, hv), F32)
    temps = GLA_HEADS * (8 * _nbytes((tt, hk), F32) + 4 * _nbytes((tt, hv), F32))
    return pl.pallas_call(
        _gla_body,
        out_shape=jax.ShapeDtypeStruct((bsz, seq, d_v), MXU_DTYPE),
        grid=(bsz, seq // tt),
        in_specs=[
            pl.BlockSpec((1, tt, d_k), lambda b, t: (b, t, 0)),
            pl.BlockSpec((1, tt, d_k), lambda b, t: (b, t, 1)),
            pl.BlockSpec((1, tt, d_v), lambda b, t: (b, t, 2 * d_k // d_v)),
            pl.BlockSpec((1, tt, d_v), lambda b, t: (b, t, 2 * d_k // d_v + 1)),
            pl.BlockSpec((1, tt, gl_w), lambda b, t: (b, t, 0)),
            pl.BlockSpec((gl_w, d_k), lambda b, t: (0, 0)),
            pl.BlockSpec((1, d_k), lambda b, t: (0, 0)),
            pl.BlockSpec((1, hv), lambda b, t: (0, 0)),
        ],
        out_specs=pl.BlockSpec((1, tt, d_v), lambda b, t: (b, t, 0)),
        scratch_shapes=[pltpu.VMEM((GLA_HEADS, hk, hv), F32)],
        compiler_params=pltpu.CompilerParams(
            dimension_semantics=("parallel", "arbitrary"),
            vmem_limit_bytes=_vmem_limit(blocks, scratch, temps)),
        name="gla_core",
    )(u, u, u, u, g_low, w_gate_up, b_gate_up.reshape(1, d_k), head_g.reshape(1, hv))


def _even_mixer(x, g, w_in, conv_w, sinks, w_out, i, bsz, seq):
    u = _norm_matmul(x, g, w_in, i, w_in.shape[2], name="even_in_proj")
    return _even_core(x, u.reshape(bsz, seq, -1), conv_w, sinks, w_out, i)


def _odd_mixer(x, g, w_in, w_gate_up, b_gate_up, head_g, w_out, i, bsz, seq):
    rank, d_k = w_gate_up.shape
    d_main = w_in.shape[2] - rank
    w_low = jnp.pad(w_in[i, :, d_main:], ((0, 0), (0, V7X_LANES - rank))).astype(MXU_DTYPE)
    w_up = jnp.pad(w_gate_up, ((0, V7X_LANES - rank), (0, 0))).astype(MXU_DTYPE)
    w_main = w_in[:, :, :d_main].astype(MXU_DTYPE)
    u, g_low = _norm_matmul(x, g, w_main, i, d_main, w_tail=w_low, name="odd_in_proj")
    o = _gla_core(u.reshape(bsz, seq, -1), g_low.reshape(bsz, seq, -1), w_up, b_gate_up, head_g)
    return _matmul_residual(x, o.reshape(bsz * seq, -1), w_out, i, name="odd_out_proj")


def kernel(x, norm_g, ffn_pre_w1, ffn_pre_w3, ffn_pre_w2, ffn_post_w1, ffn_post_w3, ffn_post_w2, even_w_in,
           even_conv_w, even_sinks, even_w_out, odd_w_in, odd_w_gate_up, odd_b_gate_up, odd_head_g, odd_w_out,
           final_g):
    bsz, seq, d = x.shape
    depth = norm_g.shape[0]
    h = x.reshape(bsz * seq, d)
    pre = (ffn_pre_w1, ffn_pre_w3, ffn_pre_w2)
    post = (ffn_post_w1, ffn_post_w3, ffn_post_w2)
    calls = [(stack, layer) for layer in range(depth) for stack in (pre, post)]
    weights = tuple(w[0:1].astype(MXU_DTYPE) for w in pre)
    even_in, even_out = even_w_in.astype(MXU_DTYPE), even_w_out.astype(MXU_DTYPE)
    odd_out = odd_w_out.astype(MXU_DTYPE)
    for c, (stack, layer) in enumerate(calls):
        last = c == len(calls) - 1
        is_pre = stack is pre
        h, weights = _ffn(h, norm_g[layer, 0 if is_pre else 2], weights, 0, final_g, final_norm=last,
                          cast_next=None if last else calls[c + 1])
        if is_pre:
            i = layer // 2
            if layer % 2 == 0:
                h = _even_mixer(h, norm_g[layer, 1], even_in, even_conv_w[i], even_sinks[i], even_out, i,
                                bsz, seq)
            else:
                h = _odd_mixer(h, norm_g[layer, 1], odd_w_in, odd_w_gate_up[i], odd_b_gate_up[i], odd_head_g[i],
                               odd_out, i, bsz, seq)
    return h.reshape(bsz, seq, d)
```

```python
import functools

import jax
import jax.numpy as jnp
import numpy as np
from jax import lax
from jax.experimental import pallas as pl
from jax.experimental.pallas import tpu as pltpu

F32 = jnp.float32
MXU_DTYPE = jnp.bfloat16

NORM_EPS = 1e-6
FFN_HALF = 0.5
CONV_WIDTH = 3
HEAD_DIM = 64
N_KV_HEADS = 4
WINDOW = 128
ATTN_BLOCK = 128
GLA_HEADS = 4
GLA_CHUNK = 128
GLA_GATE_NORMALIZER = 16.0
ATTN_LOOKAHEAD = 6
PROJECT_BEFORE_STAGING = 2

V7X_LANES = 128
V7X_SUBLANES = 8
BF16_SUBLANES = 16
V7X_MXU_DIM = 256
V7X_VMEM_BYTES = 64 * 1024 * 1024


def _vmem_limit(block_bytes, scratch_bytes, temp_bytes):
    need = 2 * block_bytes + scratch_bytes + temp_bytes
    return int(min(need, V7X_VMEM_BYTES))


def _nbytes(shape, dtype):
    return int(np.prod(shape)) * jnp.dtype(dtype).itemsize


def _tile(n, want):
    if n <= want:
        return n
    t = want
    while n % t:
        t -= V7X_SUBLANES
    return t


def _rms_norm(x, g):
    return x * lax.rsqrt(jnp.mean(x * x, axis=-1, keepdims=True) + NORM_EPS) * g


def _dot_nt(a, b):
    return lax.dot_general(a, b, (((1,), (1,)), ((), ())), preferred_element_type=F32)


def _ffn_body(x_ref, g_ref, w1_ref, w3_ref, w2_ref, fg_ref, *rest, final_norm, n_cast):
    cast_src, o_ref, cast_dst, n_ref = rest[:n_cast], rest[n_cast], rest[n_cast + 1:2 * n_cast + 1], rest[-1]
    k = pl.program_id(1)

    def half_swiglu_chunk():
        for src, dst in zip(cast_src, cast_dst):
            dst[...] = src[...].astype(dst.dtype)
        n = n_ref[...]
        a = jnp.dot(n, w1_ref[...], preferred_element_type=F32)
        b = jnp.dot(n, w3_ref[...], preferred_element_type=F32)
        h = (a * jax.nn.sigmoid(a)) * b
        return FFN_HALF * jnp.dot(h.astype(w2_ref.dtype), w2_ref[...], preferred_element_type=F32)

    @pl.when(k == 0)
    def _():
        n_ref[...] = _rms_norm(x_ref[...], g_ref[...]).astype(n_ref.dtype)
        o_ref[...] = x_ref[...] + half_swiglu_chunk()

    @pl.when(k > 0)
    def _():
        o_ref[...] += half_swiglu_chunk()

    if final_norm:
        @pl.when(k == pl.num_programs(1) - 1)
        def _():
            o_ref[...] = _rms_norm(o_ref[...], fg_ref[...])


def _ffn(x, g, weights, layer, final_g, *, final_norm, cast_next=None):
    w1, w3, w2 = weights
    n_tok, d = x.shape
    d_ff = w1.shape[2]
    tm = _tile(n_tok, 1024)
    tf = _tile(d_ff, 512)
    gi, gk = n_tok // tm, d_ff // tf
    blocks = (2 * _nbytes((tm, d), F32) + 2 * _nbytes((d, tf), MXU_DTYPE) + _nbytes((tf, d), MXU_DTYPE)
              + 2 * _nbytes((1, d), F32))
    scratch = _nbytes((tm, d), MXU_DTYPE)
    temps = 6 * _nbytes((tm, tf), F32)
    in_specs = [
        pl.BlockSpec((tm, d), lambda i, k: (i, 0)),
        pl.BlockSpec((1, d), lambda i, k: (0, 0)),
        pl.BlockSpec((None, d, tf), lambda i, k: (layer, 0, k)),
        pl.BlockSpec((None, d, tf), lambda i, k: (layer, 0, k)),
        pl.BlockSpec((None, tf, d), lambda i, k: (layer, k, 0)),
        pl.BlockSpec((1, d), lambda i, k: (0, 0)),
    ]
    args = [x, g.reshape(1, d), w1, w3, w2, final_g.reshape(1, d)]
    out_shape = [jax.ShapeDtypeStruct((n_tok, d), F32)]
    out_specs = [pl.BlockSpec((tm, d), lambda i, k: (i, 0))]
    n_cast = 0
    if cast_next is not None:
        (f1, f3, f2), nxt = cast_next
        rows_in, rows_out = d // gi, d_ff // (gi * gk)
        assert d % gi == 0 and d_ff % (gi * gk) == 0
        assert rows_in % BF16_SUBLANES == 0 and rows_out % BF16_SUBLANES == 0
        n_cast = 3
        args += [f1, f3, f2]
        in_specs += [
            pl.BlockSpec((None, rows_in, tf), lambda i, k: (nxt, i, k)),
            pl.BlockSpec((None, rows_in, tf), lambda i, k: (nxt, i, k)),
            pl.BlockSpec((None, rows_out, d), lambda i, k: (nxt, i * gk + k, 0)),
        ]
        out_shape += [jax.ShapeDtypeStruct((1, d, d_ff), MXU_DTYPE), jax.ShapeDtypeStruct((1, d, d_ff), MXU_DTYPE),
                      jax.ShapeDtypeStruct((1, d_ff, d), MXU_DTYPE)]
        out_specs += [
            pl.BlockSpec((None, rows_in, tf), lambda i, k: (0, i, k)),
            pl.BlockSpec((None, rows_in, tf), lambda i, k: (0, i, k)),
            pl.BlockSpec((None, rows_out, d), lambda i, k: (0, i * gk + k, 0)),
        ]
        blocks += 2 * (_nbytes((rows_in, tf), F32) + _nbytes((rows_in, tf), MXU_DTYPE))
        blocks += _nbytes((rows_out, d), F32) + _nbytes((rows_out, d), MXU_DTYPE)
    out = pl.pallas_call(
        functools.partial(_ffn_body, final_norm=final_norm, n_cast=n_cast),
        out_shape=out_shape,
        grid=(gi, gk),
        in_specs=in_specs,
        out_specs=out_specs,
        scratch_shapes=[pltpu.VMEM((tm, d), MXU_DTYPE)],
        compiler_params=pltpu.CompilerParams(
            dimension_semantics=("parallel", "arbitrary"),
            vmem_limit_bytes=_vmem_limit(blocks, scratch, temps)),
        name="ffn_final" if final_norm else "ffn",
    )(*args)
    return out[0], tuple(out[1:])


def _even_in_proj_body(x_ref, g_ref, w_ref, *rest, n_cast, col_tile):
    cast_src, (uc_ref, uq_ref), cast_dst, n_ref = rest[:n_cast], rest[n_cast:n_cast + 2], rest[n_cast + 2:-1], rest[-1]
    for src, dst in zip(cast_src, cast_dst):
        dst[...] = src[:, 0:dst.shape[1]].astype(dst.dtype)
    n_ref[...] = _rms_norm(x_ref[...], g_ref[...]).astype(n_ref.dtype)
    n_conv = uc_ref.shape[1]
    for c0 in range(0, n_conv, col_tile):
        uc_ref[:, c0:c0 + col_tile] = jnp.dot(n_ref[...], w_ref[:, c0:c0 + col_tile], preferred_element_type=F32)
    for c0 in range(0, uq_ref.shape[1], col_tile):
        uq_ref[:, c0:c0 + col_tile] = jnp.dot(n_ref[...], w_ref[:, n_conv + c0:n_conv + c0 + col_tile],
                                              preferred_element_type=F32).astype(uq_ref.dtype)


def _even_in_proj(x, g, w, layer, n_conv, *, cast_next=None):
    n_tok, d = x.shape
    n_out = w.shape[2]
    n_qkv = n_out - n_conv
    tm = _tile(n_tok, 512)
    col_tile = _tile(n_qkv, 1536)
    assert n_conv % col_tile == 0
    steps = n_tok // tm
    blocks = _nbytes((tm, d), F32) + _nbytes((1, d), F32) + _nbytes((tm, n_conv), F32) + _nbytes((tm, n_qkv), MXU_DTYPE)
    scratch = _nbytes((tm, d), MXU_DTYPE) + _nbytes((d, n_out), MXU_DTYPE)
    temps = 2 * _nbytes((tm, col_tile), F32)
    in_specs = [
        pl.BlockSpec((tm, d), lambda i: (i, 0)),
        pl.BlockSpec((1, d), lambda i: (0, 0)),
        pl.BlockSpec((None, d, n_out), lambda i: (layer, 0, 0), pipeline_mode=pl.Buffered(1)),
    ]
    args = [x, g.reshape(1, d), w]
    out_shape = [jax.ShapeDtypeStruct((n_tok, n_conv), F32), jax.ShapeDtypeStruct((n_tok, n_qkv), MXU_DTYPE)]
    out_specs = [pl.BlockSpec((tm, n_conv), lambda i: (i, 0)), pl.BlockSpec((tm, n_qkv), lambda i: (i, 0))]
    n_cast = 0
    if cast_next is not None:
        f, nxt, n_cols = cast_next
        rows = f.shape[1] // steps
        assert f.shape[1] % steps == 0 and rows % BF16_SUBLANES == 0 and n_cols % V7X_LANES == 0
        n_cast = 1
        args.append(f)
        in_specs.append(pl.BlockSpec((None, rows, f.shape[2]), lambda i: (nxt, i, 0)))
        out_shape.append(jax.ShapeDtypeStruct((1, f.shape[1], n_cols), MXU_DTYPE))
        out_specs.append(pl.BlockSpec((None, rows, n_cols), lambda i: (0, i, 0)))
        blocks += _nbytes((rows, f.shape[2]), F32) + _nbytes((rows, n_cols), MXU_DTYPE)
    out = pl.pallas_call(
        functools.partial(_even_in_proj_body, n_cast=n_cast, col_tile=col_tile),
        out_shape=out_shape,
        grid=(steps,),
        in_specs=in_specs,
        out_specs=out_specs,
        scratch_shapes=[pltpu.VMEM((tm, d), MXU_DTYPE)],
        compiler_params=pltpu.CompilerParams(
            dimension_semantics=("parallel",),
            vmem_limit_bytes=_vmem_limit(blocks, scratch, temps)),
        name="even_in_proj",
    )(*args)
    return out[0], out[1], (out[2] if n_cast else None)


def _odd_in_proj_body(x_ref, g_ref, w_ref, wt_ref, o_ref, ot_ref, n_ref, *, col_tile):
    n_ref[...] = _rms_norm(x_ref[...], g_ref[...]).astype(n_ref.dtype)
    ot_ref[...] = jnp.dot(n_ref[...], wt_ref[...], preferred_element_type=F32)
    for c0 in range(0, o_ref.shape[1], col_tile):
        o_ref[:, c0:c0 + col_tile] = jnp.dot(n_ref[...], w_ref[:, c0:c0 + col_tile], preferred_element_type=F32)


def _odd_in_proj(x, g, w, layer, w_tail):
    n_tok, d = x.shape
    n_out = w.shape[2]
    n_tail = w_tail.shape[1]
    tm = _tile(n_tok, 512)
    group = _tile(n_out, 3072)
    col_tile = _tile(group, 1536)
    assert n_out % group == 0 and group % col_tile == 0
    blocks = (_nbytes((tm, d), F32) + _nbytes((1, d), F32) + _nbytes((tm, group), F32)
              + _nbytes((d, n_tail), MXU_DTYPE) + _nbytes((tm, n_tail), F32))
    scratch = _nbytes((tm, d), MXU_DTYPE) + _nbytes((d, group), MXU_DTYPE)
    temps = 2 * _nbytes((tm, col_tile), F32)
    n_groups = n_out // group
    u, tail = pl.pallas_call(
        functools.partial(_odd_in_proj_body, col_tile=col_tile),
        out_shape=(jax.ShapeDtypeStruct((n_tok, n_out), F32), jax.ShapeDtypeStruct((n_groups, n_tok, n_tail), F32)),
        grid=(n_groups, n_tok // tm),
        in_specs=[
            pl.BlockSpec((tm, d), lambda c, i: (i, 0)),
            pl.BlockSpec((1, d), lambda c, i: (0, 0)),
            pl.BlockSpec((None, d, group), lambda c, i: (layer, 0, c), pipeline_mode=pl.Buffered(1)),
            pl.BlockSpec((d, n_tail), lambda c, i: (0, 0)),
        ],
        out_specs=(pl.BlockSpec((tm, group), lambda c, i: (i, c)),
                   pl.BlockSpec((None, tm, n_tail), lambda c, i: (c, i, 0))),
        scratch_shapes=[pltpu.VMEM((tm, d), MXU_DTYPE)],
        compiler_params=pltpu.CompilerParams(
            dimension_semantics=("arbitrary", "arbitrary"),
            vmem_limit_bytes=_vmem_limit(blocks, scratch, temps)),
        name="odd_in_proj",
    )(x, g.reshape(1, d), w, w_tail)
    return u, tail[0]


def _matmul_residual_body(x_ref, y_ref, w_ref, o_ref):
    o_ref[...] = x_ref[...] + jnp.dot(y_ref[...], w_ref[...], preferred_element_type=F32)


def _matmul_residual(x, y, w, layer, *, name):
    n_tok, d = x.shape
    d_in = y.shape[1]
    tm = _tile(n_tok, 1024)
    tn = _tile(d, 1024)
    blocks = (2 * _nbytes((tm, tn), F32) + _nbytes((tm, d_in), y.dtype) + _nbytes((d_in, tn), MXU_DTYPE))
    temps = _nbytes((tm, tn), F32)
    return pl.pallas_call(
        _matmul_residual_body,
        out_shape=jax.ShapeDtypeStruct((n_tok, d), F32),
        grid=(n_tok // tm, d // tn),
        in_specs=[
            pl.BlockSpec((tm, tn), lambda i, j: (i, j)),
            pl.BlockSpec((tm, d_in), lambda i, j: (i, 0)),
            pl.BlockSpec((None, d_in, tn), lambda i, j: (layer, 0, j)),
        ],
        out_specs=pl.BlockSpec((tm, tn), lambda i, j: (i, j)),
        compiler_params=pltpu.CompilerParams(
            dimension_semantics=("parallel", "arbitrary"),
            vmem_limit_bytes=_vmem_limit(blocks, 0, temps)),
        name=name,
    )(x, y, w)


def _alibi_slopes(n_heads):
    ex = np.arange(1, n_heads + 1, dtype=np.float32) * np.float32(8.0 / n_heads)
    return np.power(np.float32(2.0), -ex)


def _even_core_body(sinks_ref, ub_ref, uc_ref, ux_ref, q_ref, kv_ref, kvp_ref, ucp_ref, uxp_ref, cw_ref,
                    x_ref, w_ref, o_ref, hs_ref, ks_ref, vts_ref, bias_ref, y_ref, *, slopes, tiles_per_seq):
    tq = ub_ref.shape[1]
    d_conv = ub_ref.shape[2]
    halo = ucp_ref.shape[1]
    blk = ATTN_BLOCK
    step = pl.program_id(0)
    tile = jnp.minimum(step, pl.num_programs(0) - 2)
    first_tile = lax.rem(tile, tiles_per_seq) == 0
    n_heads = bias_ref.shape[0]
    slot = lax.rem(step, 2)
    y_cur = y_ref.at[slot]
    y_prev = y_ref.at[1 - slot]

    @pl.when(step == 0)
    def _():
        y_ref[...] = jnp.zeros_like(y_ref)

    key = lax.broadcasted_iota(jnp.int32, (2 * blk, blk), 0)
    qry = lax.broadcasted_iota(jnp.int32, (2 * blk, blk), 1)

    @pl.when(step == 0)
    def _():
        dist_i = qry + blk - key
        dist = dist_i.astype(F32)
        valid = (dist_i >= 0) & (dist_i < WINDOW)
        for head in range(n_heads):
            bias_ref[head] = jnp.where(valid, -(float(slopes[head]) * dist), -jnp.inf)

    def short_conv():
        prev = ucp_ref[0] * uxp_ref[0]
        hs_ref[0:halo, :] = jnp.where(first_tile, jnp.zeros_like(prev), prev)
        hs_ref[halo:, :] = uc_ref[0] * ux_ref[0]
        y = None
        for j in range(CONV_WIDTH):
            off = halo - (CONV_WIDTH - 1) + j
            term = cw_ref[j:j + 1, :] * hs_ref[off:off + tq, :]
            y = term if y is None else y + term
        y_cur[:, 0:d_conv] = (ub_ref[0] * y).astype(y_cur.dtype)

    def project(c):
        cols = slice(c * V7X_MXU_DIM, (c + 1) * V7X_MXU_DIM)
        o_ref[:, cols] = x_ref[:, cols] + jnp.dot(y_prev[...], w_ref[:, cols], preferred_element_type=F32)

    n_slices = o_ref.shape[1] // V7X_MXU_DIM
    for c in range(PROJECT_BEFORE_STAGING):
        project(c)

    half = V7X_LANES // 2
    low = lax.broadcasted_iota(jnp.int32, (tq + blk, V7X_LANES), 1) < half
    top = lax.broadcasted_iota(jnp.int32, (V7X_LANES, tq + blk), 0) < half
    kv_all = jnp.concatenate([kvp_ref[0], kv_ref[0]], axis=0).astype(F32)
    n_slabs = kv_all.shape[1] // (2 * V7X_LANES)
    for s in range(n_slabs):
        slab = kv_all[:, s * V7X_LANES:(s + 1) * V7X_LANES]
        swapped = pltpu.roll(slab, half, axis=1)
        zero = jnp.zeros_like(slab)
        ks_ref[2 * s, 0] = jnp.where(low, slab, zero).astype(ks_ref.dtype)
        ks_ref[2 * s, 1] = jnp.where(low, zero, swapped).astype(ks_ref.dtype)
        ks_ref[2 * s + 1, 0] = jnp.where(low, swapped, zero).astype(ks_ref.dtype)
        ks_ref[2 * s + 1, 1] = jnp.where(low, zero, slab).astype(ks_ref.dtype)
        slab_t = kv_all[:, (n_slabs + s) * V7X_LANES:(n_slabs + s + 1) * V7X_LANES].T
        swapped_t = pltpu.roll(slab_t, half, axis=0)
        zero_t = jnp.zeros_like(slab_t)
        vts_ref[2 * s, 0] = jnp.where(top, slab_t, zero_t).astype(vts_ref.dtype)
        vts_ref[2 * s, 1] = jnp.where(top, zero_t, swapped_t).astype(vts_ref.dtype)
        vts_ref[2 * s + 1, 0] = jnp.where(top, swapped_t, zero_t).astype(vts_ref.dtype)
        vts_ref[2 * s + 1, 1] = jnp.where(top, zero_t, slab_t).astype(vts_ref.dtype)

    no_history = first_tile & (key < blk)
    scale = HEAD_DIM ** -0.5
    group = n_heads // N_KV_HEADS

    def scores(qb, g):
        rows = slice(qb * blk, (qb + 1) * blk)
        keys = slice(qb * blk, (qb + 2) * blk)
        slab0 = g * group // 2
        q_slabs = [q_ref[0, rows, (slab0 + a) * V7X_LANES:(slab0 + a + 1) * V7X_LANES]
                   for a in range(group // 2)]
        q4 = (jnp.concatenate(q_slabs, axis=0) * scale).astype(MXU_DTYPE)
        k_cat = jnp.concatenate([ks_ref[g, 0, keys, :], ks_ref[g, 1, keys, :]], axis=0)
        return _dot_nt(k_cat, q4)

    def attend(qb, g, st):
        rows = slice(qb * blk, (qb + 1) * blk)
        keys = slice(qb * blk, (qb + 2) * blk)
        slab0 = g * group // 2
        p_cols = []
        for a in range(group // 2):
            p_rows = []
            for hi in range(2):
                head = g * group + 2 * a + hi
                sq = st[hi * 2 * blk:(hi + 1) * 2 * blk, a * blk:(a + 1) * blk] + bias_ref[head]
                if qb == 0:
                    sq = jnp.where(no_history, -jnp.inf, sq)
                sink = sinks_ref[head]
                m = jnp.maximum(jnp.max(sq, axis=0, keepdims=True), sink)
                e = jnp.exp(sq - m)
                denom = jnp.sum(e, axis=0, keepdims=True) + jnp.exp(sink - m)
                p_rows.append((e * (1.0 / denom)).astype(MXU_DTYPE))
            p_cols.append(jnp.concatenate(p_rows, axis=0))
        pt = jnp.concatenate(p_cols, axis=1)
        vt_cat = jnp.concatenate([vts_ref[g, 0, :, keys], vts_ref[g, 1, :, keys]], axis=1)
        out = jnp.dot(vt_cat, pt, preferred_element_type=F32).T
        for a in range(group // 2):
            c0 = d_conv + (slab0 + a) * V7X_LANES
            y_cur[rows, c0:c0 + V7X_LANES] = out[a * blk:(a + 1) * blk].astype(y_cur.dtype)

    work = [(qb, g) for qb in range(tq // blk) for g in range(N_KV_HEADS)]
    pending = [scores(*w) for w in work[:ATTN_LOOKAHEAD]]
    for c in range(PROJECT_BEFORE_STAGING, n_slices):
        project(c)
    short_conv()
    for i, w in enumerate(work):
        if i + ATTN_LOOKAHEAD < len(work):
            pending.append(scores(*work[i + ATTN_LOOKAHEAD]))
        attend(*w, pending.pop(0))


def _even_core(x, u, qkv, conv_w, sinks, w_out, layer):
    bsz, seq, _ = u.shape
    n_tok, d = x.shape
    d_conv = conv_w.shape[1]
    n_q_heads = sinks.shape[0]
    d_attn = n_q_heads * HEAD_DIM
    d_kv = N_KV_HEADS * HEAD_DIM
    d_mix = d_conv + d_attn
    assert d_conv == d_attn and (n_q_heads // N_KV_HEADS) % 2 == 0 and (2 * d_kv) % V7X_LANES == 0
    tq = _tile(seq, 512)
    assert tq % ATTN_BLOCK == 0 and d % V7X_MXU_DIM == 0
    halo = V7X_SUBLANES
    blocks_per_tile = tq // ATTN_BLOCK
    halos_per_tile = tq // halo
    tiles_per_seq = seq // tq
    n_tiles = bsz * tiles_per_seq
    kv_col = d_attn // (2 * d_kv)
    slopes = _alibi_slopes(n_q_heads)

    def mixed(s):
        t = jnp.minimum(s, n_tiles - 1)
        return t // tiles_per_seq, t % tiles_per_seq

    def cur(col):
        return lambda s: (*mixed(s), col)

    def prev(per_tile, col):
        return lambda s: (mixed(s)[0], jnp.maximum(mixed(s)[1] * per_tile - 1, 0), col)

    def projected(s):
        return jnp.maximum(s - 1, 0), 0

    blocks = (3 * _nbytes((tq, d_conv), F32) + _nbytes((tq, d_attn), qkv.dtype)
              + _nbytes((tq + ATTN_BLOCK, 2 * d_kv), qkv.dtype)
              + 2 * _nbytes((halo, d_conv), F32) + 2 * _nbytes((tq, d), F32))
    scratch = (_nbytes((tq + halo, d_conv), F32) + 4 * _nbytes((tq + ATTN_BLOCK, 2 * d_kv), MXU_DTYPE)
               + _nbytes((n_q_heads, ATTN_BLOCK, 2 * ATTN_BLOCK), F32) + 2 * _nbytes((tq, d_mix), MXU_DTYPE)
               + _nbytes((d_mix, d), MXU_DTYPE))
    temps = 3 * _nbytes((tq, d_conv), F32) + 2 * _nbytes((tq + ATTN_BLOCK, 2 * d_kv), F32)
    return pl.pallas_call(
        functools.partial(_even_core_body, slopes=slopes, tiles_per_seq=tiles_per_seq),
        out_shape=jax.ShapeDtypeStruct((n_tok, d), F32),
        grid=(n_tiles + 1,),
        in_specs=[
            pl.BlockSpec(memory_space=pltpu.SMEM),
            pl.BlockSpec((1, tq, d_conv), cur(0)),
            pl.BlockSpec((1, tq, d_conv), cur(1)),
            pl.BlockSpec((1, tq, d_conv), cur(2)),
            pl.BlockSpec((1, tq, d_attn), cur(0)),
            pl.BlockSpec((1, tq, 2 * d_kv), cur(kv_col)),
            pl.BlockSpec((1, ATTN_BLOCK, 2 * d_kv), prev(blocks_per_tile, kv_col)),
            pl.BlockSpec((1, halo, d_conv), prev(halos_per_tile, 1)),
            pl.BlockSpec((1, halo, d_conv), prev(halos_per_tile, 2)),
            pl.BlockSpec((CONV_WIDTH, d_conv), lambda s: (0, 0)),
            pl.BlockSpec((tq, d), projected),
            pl.BlockSpec((None, d_mix, d), lambda s: (layer, 0, 0), pipeline_mode=pl.Buffered(1)),
        ],
        out_specs=pl.BlockSpec((tq, d), projected),
        scratch_shapes=[
            pltpu.VMEM((tq + halo, d_conv), F32),
            pltpu.VMEM((N_KV_HEADS, 2, tq + ATTN_BLOCK, V7X_LANES), MXU_DTYPE),
            pltpu.VMEM((N_KV_HEADS, 2, V7X_LANES, tq + ATTN_BLOCK), MXU_DTYPE),
            pltpu.VMEM((n_q_heads, 2 * ATTN_BLOCK, ATTN_BLOCK), F32),
            pltpu.VMEM((2, tq, d_mix), MXU_DTYPE),
        ],
        compiler_params=pltpu.CompilerParams(
            dimension_semantics=("arbitrary",),
            vmem_limit_bytes=_vmem_limit(blocks, scratch, temps)),
        name="even_mixer",
    )(sinks, u, u, u, qkv, qkv, qkv, u, u, conv_w, x, w_out)


def _split3(x):
    hi = x.astype(MXU_DTYPE)
    r1 = x - hi.astype(F32)
    mid = r1.astype(MXU_DTYPE)
    lo = (r1 - mid.astype(F32)).astype(MXU_DTYPE)
    return hi, mid, lo


GLA_PROLOGUE_STAGES = 3


def _gla_head(load, store, g_low, keep, tri, s_ref):
    q, k, v, r, w_up, b_up, head_g = load()
    tt, hk = q.shape
    ch = GLA_CHUNK
    n_chunks = tt // ch
    sub = keep.shape[0]

    z = jnp.dot(g_low, w_up, preferred_element_type=F32) + b_up
    log_a = (jnp.minimum(z, 0.0) - jnp.log(1.0 + jnp.exp(-jnp.abs(z)))) * (1.0 / GLA_GATE_NORMALIZER)
    parts = jnp.concatenate(_split3(log_a), axis=1)
    yield

    b = []
    for a in range(0, tt, sub):
        c3 = jnp.dot(tri, parts[a:a + sub], preferred_element_type=F32)
        b.append(c3[:, 0:hk] + c3[:, hk:2 * hk] + c3[:, 2 * hk:3 * hk])
    b = jnp.concatenate(b, axis=0)
    decay = jnp.concatenate(
        [jnp.broadcast_to(jnp.exp(b[(c + 1) * ch - 1:(c + 1) * ch, :]), (ch, hk)) for c in range(n_chunks)],
        axis=0)
    grow = jnp.exp(b)
    q_dec = ((q * (hk ** -0.5)) * grow).astype(MXU_DTYPE)
    k_inv_f32 = k * (1.0 / grow)
    k_inv = k_inv_f32.astype(MXU_DTYPE)
    k_tail_t = (k_inv_f32 * decay).T.astype(MXU_DTYPE)
    decay_t = decay.T
    yield

    o_intra = []
    for a in range(0, tt, sub):
        attn = _dot_nt(q_dec[a:a + sub], k_inv[a:a + sub])
        attn = jnp.where(keep, attn, 0.0).astype(MXU_DTYPE)
        o_intra.append(jnp.dot(attn, v[a:a + sub], preferred_element_type=F32))
    o_intra = jnp.concatenate(o_intra, axis=0)
    yield

    o_inter = []
    for c in range(n_chunks):
        rows = slice(c * ch, (c + 1) * ch)
        state = s_ref[...]
        o_inter.append(jnp.dot(q_dec[rows], state.astype(MXU_DTYPE), preferred_element_type=F32))
        kv = jnp.dot(k_tail_t[:, rows], v[rows], preferred_element_type=F32)
        s_ref[...] = decay_t[:, c * ch:c * ch + 1] * state + kv
        yield
    o = o_intra + jnp.concatenate(o_inter, axis=0)

    o = _rms_norm(o, head_g)
    half_r = 0.5 * r
    store(o * (half_r + half_r * jnp.tanh(half_r)))


def _gla_body(q_ref, k_ref, v_ref, r_ref, gl_ref, wgu_ref, bgu_ref, hg_ref, o_ref, s_ref):
    tt = q_ref.shape[1]
    n_heads = s_ref.shape[0]
    hk = q_ref.shape[2] // n_heads
    hv = v_ref.shape[2] // n_heads
    ch = GLA_CHUNK
    sub = min(tt, V7X_MXU_DIM)

    @pl.when(pl.program_id(1) == 0)
    def _():
        s_ref[...] = jnp.zeros_like(s_ref)

    ri = lax.broadcasted_iota(jnp.int32, (sub, sub), 0)
    ci = lax.broadcasted_iota(jnp.int32, (sub, sub), 1)
    keep = (ri // ch == ci // ch) & (ci <= ri)
    tri = jnp.where(keep, 1.0, 0.0).astype(MXU_DTYPE)

    g_low = gl_ref[0].astype(MXU_DTYPE)

    def head(h):
        ks = slice(h * hk, (h + 1) * hk)
        vs = slice(h * hv, (h + 1) * hv)

        def load():
            return (q_ref[0, :, ks], k_ref[0, :, ks], v_ref[0, :, vs].astype(MXU_DTYPE), r_ref[0, :, vs],
                    wgu_ref[:, ks], bgu_ref[:, ks], hg_ref[...])

        def store(o):
            o_ref[0, :, vs] = o.astype(o_ref.dtype)

        return _gla_head(load, store, g_low, keep, tri, s_ref.at[h])

    heads = [head(h) for h in range(n_heads)]
    finished = object()
    for _ in range(GLA_PROLOGUE_STAGES):
        next(heads[0])
    for h in range(n_heads):
        ahead = heads[h + 1] if h + 1 < n_heads else None
        todo = GLA_PROLOGUE_STAGES if ahead is not None else 0
        for _ in range(tt // ch):
            next(heads[h])
            if todo:
                next(ahead)
                todo -= 1
        assert next(heads[h], finished) is finished
        for _ in range(todo):
            next(ahead)


def _gla_core(u, g_low, w_gate_up, b_gate_up, head_g):
    bsz, seq, _ = u.shape
    gl_w, d_k = w_gate_up.shape
    hk = d_k // GLA_HEADS
    hv = head_g.shape[0]
    d_v = hv * GLA_HEADS
    assert d_v % d_k == 0
    tt = _tile(seq, 512)
    assert GLA_CHUNK % V7X_LANES == 0 and tt % GLA_CHUNK == 0 and min(tt, V7X_MXU_DIM) % GLA_CHUNK == 0
    blocks = (2 * _nbytes((tt, d_k), F32) + 2 * _nbytes((tt, d_v), F32) + _nbytes((tt, gl_w), F32)
              + _nbytes((gl_w, d_k), MXU_DTYPE) + _nbytes((1, d_k), F32) + _nbytes((1, hv), F32)
              + _nbytes((tt, d_v), MXU_DTYPE))
    scratch = _nbytes((GLA_HEADS, hk, hv), F32)
    temps = GLA_HEADS * (8 * _nbytes((tt, hk), F32) + 4 * _nbytes((tt, hv), F32))
    return pl.pallas_call(
        _gla_body,
        out_shape=jax.ShapeDtypeStruct((bsz, seq, d_v), MXU_DTYPE),
        grid=(bsz, seq // tt),
        in_specs=[
            pl.BlockSpec((1, tt, d_k), lambda b, t: (b, t, 0)),
            pl.BlockSpec((1, tt, d_k), lambda b, t: (b, t, 1)),
            pl.BlockSpec((1, tt, d_v), lambda b, t: (b, t, 2 * d_k // d_v)),
            pl.BlockSpec((1, tt, d_v), lambda b, t: (b, t, 2 * d_k // d_v + 1)),
            pl.BlockSpec((1, tt, gl_w), lambda b, t: (b, t, 0)),
            pl.BlockSpec((gl_w, d_k), lambda b, t: (0, 0)),
            pl.BlockSpec((1, d_k), lambda b, t: (0, 0)),
            pl.BlockSpec((1, hv), lambda b, t: (0, 0)),
        ],
        out_specs=pl.BlockSpec((1, tt, d_v), lambda b, t: (b, t, 0)),
        scratch_shapes=[pltpu.VMEM((GLA_HEADS, hk, hv), F32)],
        compiler_params=pltpu.CompilerParams(
            dimension_semantics=("parallel", "arbitrary"),
            vmem_limit_bytes=_vmem_limit(blocks, scratch, temps)),
        name="gla_core",
    )(u, u, u, u, g_low, w_gate_up, b_gate_up.reshape(1, d_k), head_g.reshape(1, hv))


def _even_mixer(x, g, w_in, conv_w, sinks, w_out, i, bsz, seq, cast_next):
    u, qkv, cast = _even_in_proj(x, g, w_in, i, 3 * conv_w.shape[1], cast_next=cast_next)
    return _even_core(x, u.reshape(bsz, seq, -1), qkv.reshape(bsz, seq, -1), conv_w, sinks, w_out, i), cast


def _odd_mixer(x, g, w_in, w_main, w_gate_up, b_gate_up, head_g, w_out, i, bsz, seq):
    rank, d_k = w_gate_up.shape
    d_main = w_in.shape[2] - rank
    w_low = jnp.pad(w_in[i, :, d_main:], ((0, 0), (0, V7X_LANES - rank))).astype(MXU_DTYPE)
    w_up = jnp.pad(w_gate_up, ((0, V7X_LANES - rank), (0, 0))).astype(MXU_DTYPE)
    u, g_low = _odd_in_proj(x, g, w_main, 0, w_low)
    o = _gla_core(u.reshape(bsz, seq, -1), g_low.reshape(bsz, seq, -1), w_up, b_gate_up, head_g)
    return _matmul_residual(x, o.reshape(bsz * seq, -1), w_out, i, name="odd_out_proj")


def kernel(x, norm_g, ffn_pre_w1, ffn_pre_w3, ffn_pre_w2, ffn_post_w1, ffn_post_w3, ffn_post_w2, even_w_in,
           even_conv_w, even_sinks, even_w_out, odd_w_in, odd_w_gate_up, odd_b_gate_up, odd_head_g, odd_w_out,
           final_g):
    bsz, seq, d = x.shape
    depth = norm_g.shape[0]
    h = x.reshape(bsz * seq, d)
    pre = (ffn_pre_w1, ffn_pre_w3, ffn_pre_w2)
    post = (ffn_post_w1, ffn_post_w3, ffn_post_w2)
    calls = [(stack, layer) for layer in range(depth) for stack in (pre, post)]
    weights = tuple(w[0:1].astype(MXU_DTYPE) for w in pre)
    even_in, even_out = even_w_in.astype(MXU_DTYPE), even_w_out.astype(MXU_DTYPE)
    odd_out = odd_w_out.astype(MXU_DTYPE)
    for c, (stack, layer) in enumerate(calls):
        last = c == len(calls) - 1
        is_pre = stack is pre
        h, weights = _ffn(h, norm_g[layer, 0 if is_pre else 2], weights, 0, final_g, final_norm=last,
                          cast_next=None if last else calls[c + 1])
        if is_pre:
            i = layer // 2
            if layer % 2 == 0:
                d_main = odd_w_in.shape[2] - odd_w_gate_up.shape[1]
                cast_next = (odd_w_in, i, d_main) if layer + 1 < depth else None
                h, odd_main = _even_mixer(h, norm_g[layer, 1], even_in, even_conv_w[i], even_sinks[i], even_out, i,
                                          bsz, seq, cast_next)
            else:
                h = _odd_mixer(h, norm_g[layer, 1], odd_w_in, odd_main, odd_w_gate_up[i], odd_b_gate_up[i],
                               odd_head_g[i], odd_out, i, bsz, seq)
    return h.reshape(bsz, seq, d)
```

```python
import functools

import jax
import jax.numpy as jnp
import numpy as np
from jax import lax
from jax.experimental import pallas as pl
from jax.experimental.pallas import tpu as pltpu

F32 = jnp.float32
MXU_DTYPE = jnp.bfloat16

NORM_EPS = 1e-6
FFN_HALF = 0.5
CONV_WIDTH = 3
HEAD_DIM = 64
N_KV_HEADS = 4
WINDOW = 128
ATTN_BLOCK = 128
GLA_HEADS = 4
GLA_CHUNK = 128
GLA_GATE_NORMALIZER = 16.0
ATTN_LOOKAHEAD = 16
PROJECT_BEFORE_STAGING = 1

V7X_LANES = 128
V7X_SUBLANES = 8
BF16_SUBLANES = 16
V7X_MXU_DIM = 256
V7X_VMEM_BYTES = 64 * 1024 * 1024


def _vmem_limit(block_bytes, scratch_bytes, temp_bytes):
    need = 2 * block_bytes + scratch_bytes + temp_bytes
    return int(min(need, V7X_VMEM_BYTES))


def _nbytes(shape, dtype):
    return int(np.prod(shape)) * jnp.dtype(dtype).itemsize


def _tile(n, want):
    if n <= want:
        return n
    t = want
    while n % t:
        t -= V7X_SUBLANES
    return t


def _rms_norm(x, g):
    return x * lax.rsqrt(jnp.mean(x * x, axis=-1, keepdims=True) + NORM_EPS) * g


def _dot_nt(a, b):
    return lax.dot_general(a, b, (((1,), (1,)), ((), ())), preferred_element_type=F32)


def _ffn_body(x_ref, g_ref, w1_ref, w3_ref, w2_ref, fg_ref, *rest, final_norm, n_cast):
    cast_src, o_ref, cast_dst, n_ref = rest[:n_cast], rest[n_cast], rest[n_cast + 1:2 * n_cast + 1], rest[-1]
    k = pl.program_id(1)

    def half_swiglu_chunk():
        for src, dst in zip(cast_src, cast_dst):
            dst[...] = src[...].astype(dst.dtype)
        n = n_ref[...]
        a = jnp.dot(n, w1_ref[...], preferred_element_type=F32)
        b = jnp.dot(n, w3_ref[...], preferred_element_type=F32)
        h = (a * jax.nn.sigmoid(a)) * b
        return FFN_HALF * jnp.dot(h.astype(w2_ref.dtype), w2_ref[...], preferred_element_type=F32)

    @pl.when(k == 0)
    def _():
        n_ref[...] = _rms_norm(x_ref[...], g_ref[...]).astype(n_ref.dtype)
        o_ref[...] = x_ref[...] + half_swiglu_chunk()

    @pl.when(k > 0)
    def _():
        o_ref[...] += half_swiglu_chunk()

    if final_norm:
        @pl.when(k == pl.num_programs(1) - 1)
        def _():
            o_ref[...] = _rms_norm(o_ref[...], fg_ref[...])


def _ffn(x, g, weights, layer, final_g, *, final_norm, cast_next=None):
    w1, w3, w2 = weights
    n_tok, d = x.shape
    d_ff = w1.shape[2]
    tm = _tile(n_tok, 1024)
    tf = _tile(d_ff, 512)
    gi, gk = n_tok // tm, d_ff // tf
    blocks = (2 * _nbytes((tm, d), F32) + 2 * _nbytes((d, tf), MXU_DTYPE) + _nbytes((tf, d), MXU_DTYPE)
              + 2 * _nbytes((1, d), F32))
    scratch = _nbytes((tm, d), MXU_DTYPE)
    temps = 6 * _nbytes((tm, tf), F32)
    in_specs = [
        pl.BlockSpec((tm, d), lambda i, k: (i, 0)),
        pl.BlockSpec((1, d), lambda i, k: (0, 0)),
        pl.BlockSpec((None, d, tf), lambda i, k: (layer, 0, k)),
        pl.BlockSpec((None, d, tf), lambda i, k: (layer, 0, k)),
        pl.BlockSpec((None, tf, d), lambda i, k: (layer, k, 0)),
        pl.BlockSpec((1, d), lambda i, k: (0, 0)),
    ]
    args = [x, g.reshape(1, d), w1, w3, w2, final_g.reshape(1, d)]
    out_shape = [jax.ShapeDtypeStruct((n_tok, d), F32)]
    out_specs = [pl.BlockSpec((tm, d), lambda i, k: (i, 0))]
    n_cast = 0
    if cast_next is not None:
        (f1, f3, f2), nxt = cast_next
        rows_in, rows_out = d // gi, d_ff // (gi * gk)
        assert d % gi == 0 and d_ff % (gi * gk) == 0
        assert rows_in % BF16_SUBLANES == 0 and rows_out % BF16_SUBLANES == 0
        n_cast = 3
        args += [f1, f3, f2]
        in_specs += [
            pl.BlockSpec((None, rows_in, tf), lambda i, k: (nxt, i, k)),
            pl.BlockSpec((None, rows_in, tf), lambda i, k: (nxt, i, k)),
            pl.BlockSpec((None, rows_out, d), lambda i, k: (nxt, i * gk + k, 0)),
        ]
        out_shape += [jax.ShapeDtypeStruct((1, d, d_ff), MXU_DTYPE), jax.ShapeDtypeStruct((1, d, d_ff), MXU_DTYPE),
                      jax.ShapeDtypeStruct((1, d_ff, d), MXU_DTYPE)]
        out_specs += [
            pl.BlockSpec((None, rows_in, tf), lambda i, k: (0, i, k)),
            pl.BlockSpec((None, rows_in, tf), lambda i, k: (0, i, k)),
            pl.BlockSpec((None, rows_out, d), lambda i, k: (0, i * gk + k, 0)),
        ]
        blocks += 2 * (_nbytes((rows_in, tf), F32) + _nbytes((rows_in, tf), MXU_DTYPE))
        blocks += _nbytes((rows_out, d), F32) + _nbytes((rows_out, d), MXU_DTYPE)
    out = pl.pallas_call(
        functools.partial(_ffn_body, final_norm=final_norm, n_cast=n_cast),
        out_shape=out_shape,
        grid=(gi, gk),
        in_specs=in_specs,
        out_specs=out_specs,
        scratch_shapes=[pltpu.VMEM((tm, d), MXU_DTYPE)],
        compiler_params=pltpu.CompilerParams(
            dimension_semantics=("parallel", "arbitrary"),
            vmem_limit_bytes=_vmem_limit(blocks, scratch, temps)),
        name="ffn_final" if final_norm else "ffn",
    )(*args)
    return out[0], tuple(out[1:])


def _even_in_proj_body(x_ref, g_ref, w_ref, cw_ref, *rest, n_cast, col_tile, tiles_per_seq):
    cast_src = rest[:n_cast]
    yc_ref, uq_ref = rest[n_cast:n_cast + 2]
    cast_dst = rest[n_cast + 2:2 * n_cast + 2]
    n_ref, hs_ref, carry_ref = rest[2 * n_cast + 2:]
    for src, dst in zip(cast_src, cast_dst):
        dst[...] = src[:, 0:dst.shape[1]].astype(dst.dtype)
    tm, d_conv = yc_ref.shape
    halo = carry_ref.shape[0]
    first_tile = lax.rem(pl.program_id(0), tiles_per_seq) == 0

    n_ref[...] = _rms_norm(x_ref[...], g_ref[...]).astype(n_ref.dtype)
    def project(c0, width):
        return jnp.dot(n_ref[...], w_ref[:, c0:c0 + width], preferred_element_type=F32)

    h = project(d_conv, d_conv) * project(2 * d_conv, d_conv)
    hs_ref[0:halo, :] = jnp.where(first_tile, jnp.zeros_like(carry_ref), carry_ref[...])
    hs_ref[halo:, :] = h
    carry_ref[...] = h[tm - halo:, :]
    n_conv = 3 * d_conv
    for c0 in range(0, uq_ref.shape[1], col_tile):
        uq_ref[:, c0:c0 + col_tile] = project(n_conv + c0, col_tile).astype(uq_ref.dtype)
    y = None
    for j in range(CONV_WIDTH):
        off = halo - (CONV_WIDTH - 1) + j
        term = cw_ref[j:j + 1, :] * hs_ref[off:off + tm, :]
        y = term if y is None else y + term
    yc_ref[...] = (project(0, d_conv) * y).astype(yc_ref.dtype)


def _even_in_proj(x, g, w, layer, conv_w, seq, *, casts=()):
    n_tok, d = x.shape
    n_out = w.shape[2]
    d_conv = conv_w.shape[1]
    n_qkv = n_out - 3 * d_conv
    tm = _tile(seq, 512)
    halo = V7X_SUBLANES
    assert halo >= CONV_WIDTH - 1 and tm % halo == 0
    col_tile = _tile(n_qkv, 1536)
    steps = n_tok // tm
    blocks = (_nbytes((tm, d), F32) + _nbytes((1, d), F32) + _nbytes((CONV_WIDTH, d_conv), F32)
              + _nbytes((tm, d_conv), MXU_DTYPE) + _nbytes((tm, n_qkv), MXU_DTYPE))
    scratch = (_nbytes((tm, d), MXU_DTYPE) + _nbytes((d, n_out), MXU_DTYPE) + _nbytes((tm + halo, d_conv), F32)
               + _nbytes((halo, d_conv), F32))
    temps = 5 * _nbytes((tm, d_conv), F32) + _nbytes((tm, col_tile), F32)
    in_specs = [
        pl.BlockSpec((tm, d), lambda i: (i, 0)),
        pl.BlockSpec((1, d), lambda i: (0, 0)),
        pl.BlockSpec((None, d, n_out), lambda i: (layer, 0, 0), pipeline_mode=pl.Buffered(1)),
        pl.BlockSpec((CONV_WIDTH, d_conv), lambda i: (0, 0)),
    ]
    args = [x, g.reshape(1, d), w, conv_w]
    out_shape = [jax.ShapeDtypeStruct((n_tok, d_conv), MXU_DTYPE), jax.ShapeDtypeStruct((n_tok, n_qkv), MXU_DTYPE)]
    out_specs = [pl.BlockSpec((tm, d_conv), lambda i: (i, 0)), pl.BlockSpec((tm, n_qkv), lambda i: (i, 0))]
    for f, nxt, n_cols in casts:
        rows = f.shape[1] // steps
        assert f.shape[1] % steps == 0 and rows % BF16_SUBLANES == 0 and n_cols % V7X_LANES == 0
        args.append(f)
        in_specs.append(pl.BlockSpec((None, rows, f.shape[2]), functools.partial(lambda i, nxt: (nxt, i, 0), nxt=nxt)))
        blocks += _nbytes((rows, f.shape[2]), F32) + _nbytes((rows, n_cols), MXU_DTYPE)
    for f, nxt, n_cols in casts:
        rows = f.shape[1] // steps
        out_shape.append(jax.ShapeDtypeStruct((1, f.shape[1], n_cols), MXU_DTYPE))
        out_specs.append(pl.BlockSpec((None, rows, n_cols), lambda i: (0, i, 0)))
    out = pl.pallas_call(
        functools.partial(_even_in_proj_body, n_cast=len(casts), col_tile=col_tile, tiles_per_seq=seq // tm),
        out_shape=out_shape,
        grid=(steps,),
        in_specs=in_specs,
        out_specs=out_specs,
        scratch_shapes=[pltpu.VMEM((tm, d), MXU_DTYPE), pltpu.VMEM((tm + halo, d_conv), F32),
                        pltpu.VMEM((halo, d_conv), F32)],
        compiler_params=pltpu.CompilerParams(
            dimension_semantics=("arbitrary",),
            vmem_limit_bytes=_vmem_limit(blocks, scratch, temps)),
        name="even_in_proj",
    )(*args)
    return out[0], out[1], list(out[2:])


def _odd_in_proj_body(x_ref, g_ref, w_ref, wt_ref, o_ref, ot_ref, n_ref, *, col_tile):
    n_ref[...] = _rms_norm(x_ref[...], g_ref[...]).astype(n_ref.dtype)
    for c0 in range(0, o_ref.shape[1], col_tile):
        o_ref[:, c0:c0 + col_tile] = jnp.dot(n_ref[...], w_ref[:, c0:c0 + col_tile], preferred_element_type=F32)
    ot_ref[...] = jnp.dot(n_ref[...], wt_ref[...], preferred_element_type=F32)


def _odd_in_proj(x, g, w, layer, w_tail):
    n_tok, d = x.shape
    n_out = w.shape[2]
    n_tail = w_tail.shape[1]
    tm = _tile(n_tok, V7X_MXU_DIM)
    col_tile = _tile(n_out, 1536)
    assert n_out % col_tile == 0
    blocks = (_nbytes((tm, d), F32) + _nbytes((1, d), F32) + _nbytes((tm, n_out), F32)
              + _nbytes((d, n_tail), MXU_DTYPE) + _nbytes((tm, n_tail), F32))
    scratch = _nbytes((tm, d), MXU_DTYPE) + _nbytes((d, n_out), MXU_DTYPE)
    temps = 2 * _nbytes((tm, col_tile), F32)
    return pl.pallas_call(
        functools.partial(_odd_in_proj_body, col_tile=col_tile),
        out_shape=(jax.ShapeDtypeStruct((n_tok, n_out), F32), jax.ShapeDtypeStruct((n_tok, n_tail), F32)),
        grid=(n_tok // tm,),
        in_specs=[
            pl.BlockSpec((tm, d), lambda i: (i, 0)),
            pl.BlockSpec((1, d), lambda i: (0, 0)),
            pl.BlockSpec((None, d, n_out), lambda i: (layer, 0, 0), pipeline_mode=pl.Buffered(1)),
            pl.BlockSpec((d, n_tail), lambda i: (0, 0)),
        ],
        out_specs=(pl.BlockSpec((tm, n_out), lambda i: (i, 0)), pl.BlockSpec((tm, n_tail), lambda i: (i, 0))),
        scratch_shapes=[pltpu.VMEM((tm, d), MXU_DTYPE)],
        compiler_params=pltpu.CompilerParams(
            dimension_semantics=("parallel",),
            vmem_limit_bytes=_vmem_limit(blocks, scratch, temps)),
        name="odd_in_proj",
    )(x, g.reshape(1, d), w, w_tail)


def _matmul_residual_body(x_ref, y_ref, w_ref, o_ref, *, col_tile):
    for c0 in range(0, o_ref.shape[1], col_tile):
        cols = slice(c0, c0 + col_tile)
        o_ref[:, cols] = x_ref[:, cols] + jnp.dot(y_ref[...], w_ref[:, cols], preferred_element_type=F32)


def _matmul_residual(x, y, w, layer, *, name):
    n_tok, d = x.shape
    d_in = y.shape[1]
    tm = _tile(n_tok, 512)
    col_tile = _tile(d, 1024)
    assert d % col_tile == 0
    blocks = 2 * _nbytes((tm, d), F32) + _nbytes((tm, d_in), y.dtype)
    scratch = _nbytes((d_in, d), MXU_DTYPE)
    temps = 2 * _nbytes((tm, col_tile), F32)
    return pl.pallas_call(
        functools.partial(_matmul_residual_body, col_tile=col_tile),
        out_shape=jax.ShapeDtypeStruct((n_tok, d), F32),
        grid=(n_tok // tm,),
        in_specs=[
            pl.BlockSpec((tm, d), lambda i: (i, 0)),
            pl.BlockSpec((tm, d_in), lambda i: (i, 0)),
            pl.BlockSpec((None, d_in, d), lambda i: (layer, 0, 0), pipeline_mode=pl.Buffered(1)),
        ],
        out_specs=pl.BlockSpec((tm, d), lambda i: (i, 0)),
        compiler_params=pltpu.CompilerParams(
            dimension_semantics=("parallel",),
            vmem_limit_bytes=_vmem_limit(blocks, scratch, temps)),
        name=name,
    )(x, y, w)


def _alibi_slopes(n_heads):
    ex = np.arange(1, n_heads + 1, dtype=np.float32) * np.float32(8.0 / n_heads)
    return np.power(np.float32(2.0), -ex)


def _even_core_body(*refs, slopes, tiles_per_seq):
    *io_refs, ya_ref, yb_ref = refs
    step = pl.program_id(0)

    @pl.when(step == 0)
    def _():
        yb_ref[...] = jnp.zeros_like(yb_ref)

    @pl.when(lax.rem(step, 2) == 0)
    def _():
        _even_mix_step(*io_refs, ya_ref, yb_ref, slopes=slopes, tiles_per_seq=tiles_per_seq)

    @pl.when(lax.rem(step, 2) == 1)
    def _():
        _even_mix_step(*io_refs, yb_ref, ya_ref, slopes=slopes, tiles_per_seq=tiles_per_seq)


def _even_mix_step(sinks_ref, yc_ref, q_ref, kv_ref, kvp_ref, x_ref, w_ref, o_ref, ks_ref, vts_ref, bias_ref,
                   y_cur, y_prev, *, slopes, tiles_per_seq):
    tq = yc_ref.shape[1]
    d_conv = yc_ref.shape[2]
    blk = ATTN_BLOCK
    step = pl.program_id(0)
    tile = jnp.minimum(step, pl.num_programs(0) - 2)
    first_tile = lax.rem(tile, tiles_per_seq) == 0
    n_heads = bias_ref.shape[0]

    key = lax.broadcasted_iota(jnp.int32, (2 * blk, blk), 0)
    qry = lax.broadcasted_iota(jnp.int32, (2 * blk, blk), 1)

    @pl.when(step == 0)
    def _():
        dist_i = qry + blk - key
        dist = dist_i.astype(F32)
        valid = (dist_i >= 0) & (dist_i < WINDOW)
        for head in range(n_heads):
            bias_ref[head] = jnp.where(valid, -(float(slopes[head]) * dist), -jnp.inf)

    def project(c):
        cols = slice(c * V7X_MXU_DIM, (c + 1) * V7X_MXU_DIM)
        o_ref[:, cols] = x_ref[:, cols] + jnp.dot(y_prev[...], w_ref[:, cols], preferred_element_type=F32)

    n_slices = o_ref.shape[1] // V7X_MXU_DIM
    for c in range(PROJECT_BEFORE_STAGING):
        project(c)

    half = V7X_LANES // 2
    low = lax.broadcasted_iota(jnp.int32, (tq + blk, V7X_LANES), 1) < half
    top = lax.broadcasted_iota(jnp.int32, (V7X_LANES, tq + blk), 0) < half
    kv_all = jnp.concatenate([kvp_ref[0], kv_ref[0]], axis=0).astype(F32)
    n_slabs = kv_all.shape[1] // (2 * V7X_LANES)
    for s in range(n_slabs):
        slab = kv_all[:, s * V7X_LANES:(s + 1) * V7X_LANES]
        swapped = pltpu.roll(slab, half, axis=1)
        zero = jnp.zeros_like(slab)
        ks_ref[2 * s, 0] = jnp.where(low, slab, zero).astype(ks_ref.dtype)
        ks_ref[2 * s, 1] = jnp.where(low, zero, swapped).astype(ks_ref.dtype)
        ks_ref[2 * s + 1, 0] = jnp.where(low, swapped, zero).astype(ks_ref.dtype)
        ks_ref[2 * s + 1, 1] = jnp.where(low, zero, slab).astype(ks_ref.dtype)
        slab_t = kv_all[:, (n_slabs + s) * V7X_LANES:(n_slabs + s + 1) * V7X_LANES].T
        swapped_t = pltpu.roll(slab_t, half, axis=0)
        zero_t = jnp.zeros_like(slab_t)
        vts_ref[2 * s, 0] = jnp.where(top, slab_t, zero_t).astype(vts_ref.dtype)
        vts_ref[2 * s, 1] = jnp.where(top, zero_t, swapped_t).astype(vts_ref.dtype)
        vts_ref[2 * s + 1, 0] = jnp.where(top, swapped_t, zero_t).astype(vts_ref.dtype)
        vts_ref[2 * s + 1, 1] = jnp.where(top, zero_t, slab_t).astype(vts_ref.dtype)

    no_history = first_tile & (key < blk)
    scale = HEAD_DIM ** -0.5
    group = n_heads // N_KV_HEADS

    def scores(qb, g):
        rows = slice(qb * blk, (qb + 1) * blk)
        keys = slice(qb * blk, (qb + 2) * blk)
        slab0 = g * group // 2
        q_slabs = [q_ref[0, rows, (slab0 + a) * V7X_LANES:(slab0 + a + 1) * V7X_LANES]
                   for a in range(group // 2)]
        q4 = (jnp.concatenate(q_slabs, axis=0) * scale).astype(MXU_DTYPE)
        k_cat = jnp.concatenate([ks_ref[g, 0, keys, :], ks_ref[g, 1, keys, :]], axis=0)
        return _dot_nt(k_cat, q4)

    def attend(qb, g, st):
        rows = slice(qb * blk, (qb + 1) * blk)
        keys = slice(qb * blk, (qb + 2) * blk)
        slab0 = g * group // 2
        p_cols = []
        for a in range(group // 2):
            p_rows = []
            for hi in range(2):
                head = g * group + 2 * a + hi
                sq = st[hi * 2 * blk:(hi + 1) * 2 * blk, a * blk:(a + 1) * blk] + bias_ref[head]
                if qb == 0:
                    sq = jnp.where(no_history, -jnp.inf, sq)
                sink = sinks_ref[head]
                m = jnp.maximum(jnp.max(sq, axis=0, keepdims=True), sink)
                e = jnp.exp(sq - m)
                denom = jnp.sum(e, axis=0, keepdims=True) + jnp.exp(sink - m)
                p_rows.append((e * (1.0 / denom)).astype(MXU_DTYPE))
            p_cols.append(jnp.concatenate(p_rows, axis=0))
        pt = jnp.concatenate(p_cols, axis=1)
        vt_cat = jnp.concatenate([vts_ref[g, 0, :, keys], vts_ref[g, 1, :, keys]], axis=1)
        out = jnp.dot(vt_cat, pt, preferred_element_type=F32).T
        for a in range(group // 2):
            c0 = d_conv + (slab0 + a) * V7X_LANES
            y_cur[rows, c0:c0 + V7X_LANES] = out[a * blk:(a + 1) * blk].astype(y_cur.dtype)

    work = [(qb, g) for qb in range(tq // blk) for g in range(N_KV_HEADS)]
    pending = [scores(*w) for w in work[:ATTN_LOOKAHEAD]]
    y_cur[:, 0:d_conv] = yc_ref[0]
    late = list(range(PROJECT_BEFORE_STAGING, n_slices))
    for i, w in enumerate(work):
        if i + ATTN_LOOKAHEAD < len(work):
            pending.append(scores(*work[i + ATTN_LOOKAHEAD]))
        attend(*w, pending.pop(0))
        if late:
            project(late.pop(0))
    for c in late:
        project(c)


def _even_core(x, y_conv, qkv, sinks, w_out, layer):
    bsz, seq, d_conv = y_conv.shape
    n_tok, d = x.shape
    n_q_heads = sinks.shape[0]
    d_attn = n_q_heads * HEAD_DIM
    d_kv = N_KV_HEADS * HEAD_DIM
    d_mix = d_conv + d_attn
    assert (n_q_heads // N_KV_HEADS) % 2 == 0 and (2 * d_kv) % V7X_LANES == 0 and d_attn % (2 * d_kv) == 0
    tq = _tile(seq, 512)
    assert tq % ATTN_BLOCK == 0 and d % V7X_MXU_DIM == 0
    blocks_per_tile = tq // ATTN_BLOCK
    tiles_per_seq = seq // tq
    n_tiles = bsz * tiles_per_seq
    kv_col = d_attn // (2 * d_kv)
    slopes = _alibi_slopes(n_q_heads)

    def mixed(s):
        t = jnp.minimum(s, n_tiles - 1)
        return t // tiles_per_seq, t % tiles_per_seq

    def cur(col):
        return lambda s: (*mixed(s), col)

    def prev_block(s):
        return mixed(s)[0], jnp.maximum(mixed(s)[1] * blocks_per_tile - 1, 0), kv_col

    def projected(s):
        return jnp.maximum(s - 1, 0), 0

    blocks = (_nbytes((tq, d_conv), y_conv.dtype) + _nbytes((tq, d_attn), qkv.dtype)
              + _nbytes((tq + ATTN_BLOCK, 2 * d_kv), qkv.dtype) + 2 * _nbytes((tq, d), F32))
    scratch = (4 * _nbytes((tq + ATTN_BLOCK, 2 * d_kv), MXU_DTYPE)
               + _nbytes((n_q_heads, ATTN_BLOCK, 2 * ATTN_BLOCK), F32) + 2 * _nbytes((tq, d_mix), MXU_DTYPE)
               + _nbytes((d_mix, d), MXU_DTYPE))
    score_tile = _nbytes((4 * ATTN_BLOCK, 2 * ATTN_BLOCK), F32)
    temps = (2 * _nbytes((tq + ATTN_BLOCK, 2 * d_kv), F32) + (ATTN_LOOKAHEAD + 6) * score_tile
             + 2 * _nbytes((tq, V7X_MXU_DIM), F32))
    return pl.pallas_call(
        functools.partial(_even_core_body, slopes=slopes, tiles_per_seq=tiles_per_seq),
        out_shape=jax.ShapeDtypeStruct((n_tok, d), F32),
        grid=(n_tiles + 1,),
        in_specs=[
            pl.BlockSpec(memory_space=pltpu.SMEM),
            pl.BlockSpec((1, tq, d_conv), cur(0)),
            pl.BlockSpec((1, tq, d_attn), cur(0)),
            pl.BlockSpec((1, tq, 2 * d_kv), cur(kv_col)),
            pl.BlockSpec((1, ATTN_BLOCK, 2 * d_kv), prev_block),
            pl.BlockSpec((tq, d), projected),
            pl.BlockSpec((None, d_mix, d), lambda s: (layer, 0, 0), pipeline_mode=pl.Buffered(1)),
        ],
        out_specs=pl.BlockSpec((tq, d), projected),
        scratch_shapes=[
            pltpu.VMEM((N_KV_HEADS, 2, tq + ATTN_BLOCK, V7X_LANES), MXU_DTYPE),
            pltpu.VMEM((N_KV_HEADS, 2, V7X_LANES, tq + ATTN_BLOCK), MXU_DTYPE),
            pltpu.VMEM((n_q_heads, 2 * ATTN_BLOCK, ATTN_BLOCK), F32),
            pltpu.VMEM((tq, d_mix), MXU_DTYPE),
            pltpu.VMEM((tq, d_mix), MXU_DTYPE),
        ],
        compiler_params=pltpu.CompilerParams(
            dimension_semantics=("arbitrary",),
            vmem_limit_bytes=_vmem_limit(blocks, scratch, temps)),
        name="even_mixer",
    )(sinks, y_conv, qkv, qkv, qkv, x, w_out)


def _split3(x):
    hi = x.astype(MXU_DTYPE)
    r1 = x - hi.astype(F32)
    mid = r1.astype(MXU_DTYPE)
    lo = (r1 - mid.astype(F32)).astype(MXU_DTYPE)
    return hi, mid, lo


GLA_PROLOGUE_STAGES = 3


def _gla_head(load, store, g_low, keep, tri, s_ref):
    q, k, v, r, w_up, b_up, head_g = load()
    tt, hk = q.shape
    ch = GLA_CHUNK
    n_chunks = tt // ch
    sub = keep.shape[0]

    z = jnp.dot(g_low, w_up, preferred_element_type=F32) + b_up
    log_a = (jnp.minimum(z, 0.0) - jnp.log(1.0 + jnp.exp(-jnp.abs(z)))) * (1.0 / GLA_GATE_NORMALIZER)
    parts = jnp.concatenate(_split3(log_a), axis=1)
    yield

    b = []
    for a in range(0, tt, sub):
        c3 = jnp.dot(tri, parts[a:a + sub], preferred_element_type=F32)
        b.append(c3[:, 0:hk] + c3[:, hk:2 * hk] + c3[:, 2 * hk:3 * hk])
    b = jnp.concatenate(b, axis=0)
    decay = jnp.concatenate(
        [jnp.broadcast_to(jnp.exp(b[(c + 1) * ch - 1:(c + 1) * ch, :]), (ch, hk)) for c in range(n_chunks)],
        axis=0)
    grow = jnp.exp(b)
    q_dec = ((q * (hk ** -0.5)) * grow).astype(MXU_DTYPE)
    k_inv_f32 = k * (1.0 / grow)
    k_inv = k_inv_f32.astype(MXU_DTYPE)
    k_tail_t = (k_inv_f32 * decay).T.astype(MXU_DTYPE)
    decay_t = decay.T
    yield

    o_intra = []
    for a in range(0, tt, sub):
        attn = _dot_nt(q_dec[a:a + sub], k_inv[a:a + sub])
        attn = jnp.where(keep, attn, 0.0).astype(MXU_DTYPE)
        o_intra.append(jnp.dot(attn, v[a:a + sub], preferred_element_type=F32))
    o_intra = jnp.concatenate(o_intra, axis=0)
    yield

    o_inter = []
    for c in range(n_chunks):
        rows = slice(c * ch, (c + 1) * ch)
        state = s_ref[...]
        o_inter.append(jnp.dot(q_dec[rows], state.astype(MXU_DTYPE), preferred_element_type=F32))
        kv = jnp.dot(k_tail_t[:, rows], v[rows], preferred_element_type=F32)
        s_ref[...] = decay_t[:, c * ch:c * ch + 1] * state + kv
        yield
    o = o_intra + jnp.concatenate(o_inter, axis=0)

    o = _rms_norm(o, head_g)
    half_r = 0.5 * r
    store(o * (half_r + half_r * jnp.tanh(half_r)))


def _gla_body(q_ref, k_ref, v_ref, r_ref, gl_ref, wgu_ref, bgu_ref, hg_ref, o_ref, s_ref):
    tt = q_ref.shape[1]
    n_heads = s_ref.shape[0]
    hk = q_ref.shape[2] // n_heads
    hv = v_ref.shape[2] // n_heads
    ch = GLA_CHUNK
    sub = min(tt, V7X_MXU_DIM)

    @pl.when(pl.program_id(1) == 0)
    def _():
        s_ref[...] = jnp.zeros_like(s_ref)

    ri = lax.broadcasted_iota(jnp.int32, (sub, sub), 0)
    ci = lax.broadcasted_iota(jnp.int32, (sub, sub), 1)
    keep = (ri // ch == ci // ch) & (ci <= ri)
    tri = jnp.where(keep, 1.0, 0.0).astype(MXU_DTYPE)

    g_low = gl_ref[0].astype(MXU_DTYPE)

    def head(h):
        ks = slice(h * hk, (h + 1) * hk)
        vs = slice(h * hv, (h + 1) * hv)

        def load():
            return (q_ref[0, :, ks], k_ref[0, :, ks], v_ref[0, :, vs].astype(MXU_DTYPE), r_ref[0, :, vs],
                    wgu_ref[:, ks], bgu_ref[:, ks], hg_ref[...])

        def store(o):
            o_ref[0, :, vs] = o.astype(o_ref.dtype)

        return _gla_head(load, store, g_low, keep, tri, s_ref.at[h])

    heads = [head(h) for h in range(n_heads)]
    finished = object()
    for _ in range(GLA_PROLOGUE_STAGES):
        next(heads[0])
    for h in range(n_heads):
        ahead = heads[h + 1] if h + 1 < n_heads else None
        todo = GLA_PROLOGUE_STAGES if ahead is not None else 0
        for _ in range(tt // ch):
            next(heads[h])
            if todo:
                next(ahead)
                todo -= 1
        assert next(heads[h], finished) is finished
        for _ in range(todo):
            next(ahead)


def _gla_core(u, g_low, w_gate_up, b_gate_up, head_g):
    bsz, seq, _ = u.shape
    gl_w, d_k = w_gate_up.shape
    hk = d_k // GLA_HEADS
    hv = head_g.shape[0]
    d_v = hv * GLA_HEADS
    assert d_v % d_k == 0
    tt = _tile(seq, 512)
    assert GLA_CHUNK % V7X_LANES == 0 and tt % GLA_CHUNK == 0 and min(tt, V7X_MXU_DIM) % GLA_CHUNK == 0
    blocks = (2 * _nbytes((tt, d_k), F32) + 2 * _nbytes((tt, d_v), F32) + _nbytes((tt, gl_w), F32)
              + _nbytes((gl_w, d_k), MXU_DTYPE) + _nbytes((1, d_k), F32) + _nbytes((1, hv), F32)
              + _nbytes((tt, d_v), MXU_DTYPE))
    scratch = _nbytes((GLA_HEADS, hk, hv), F32)
    temps = GLA_HEADS * (8 * _nbytes((tt, hk), F32) + 4 * _nbytes((tt, hv), F32))
    return pl.pallas_call(
        _gla_body,
        out_shape=jax.ShapeDtypeStruct((bsz, seq, d_v), MXU_DTYPE),
        grid=(bsz, seq // tt),
        in_specs=[
            pl.BlockSpec((1, tt, d_k), lambda b, t: (b, t, 0)),
            pl.BlockSpec((1, tt, d_k), lambda b, t: (b, t, 1)),
            pl.BlockSpec((1, tt, d_v), lambda b, t: (b, t, 2 * d_k // d_v)),
            pl.BlockSpec((1, tt, d_v), lambda b, t: (b, t, 2 * d_k // d_v + 1)),
            pl.BlockSpec((1, tt, gl_w), lambda b, t: (b, t, 0)),
            pl.BlockSpec((gl_w, d_k), lambda b, t: (0, 0)),
            pl.BlockSpec((1, d_k), lambda b, t: (0, 0)),
            pl.BlockSpec((1, hv), lambda b, t: (0, 0)),
        ],
        out_specs=pl.BlockSpec((1, tt, d_v), lambda b, t: (b, t, 0)),
        scratch_shapes=[pltpu.VMEM((GLA_HEADS, hk, hv), F32)],
        compiler_params=pltpu.CompilerParams(
            dimension_semantics=("parallel", "arbitrary"),
            vmem_limit_bytes=_vmem_limit(blocks, scratch, temps)),
        name="gla_core",
    )(u, u, u, u, g_low, w_gate_up, b_gate_up.reshape(1, d_k), head_g.reshape(1, hv))


def _even_mixer(x, g, w_in, conv_w, sinks, w_out, i, bsz, seq, casts):
    y_conv, qkv, converted = _even_in_proj(x, g, w_in, i, conv_w, seq,
                                           casts=[(w_out, i, w_out.shape[2])] + list(casts))
    h = _even_core(x, y_conv.reshape(bsz, seq, -1), qkv.reshape(bsz, seq, -1), sinks, converted[0], 0)
    return h, converted[1:]


def _odd_mixer(x, g, w_in, w_main, w_gate_up, b_gate_up, head_g, w_out, i, bsz, seq):
    rank, d_k = w_gate_up.shape
    d_main = w_in.shape[2] - rank
    w_low = jnp.pad(w_in[i, :, d_main:], ((0, 0), (0, V7X_LANES - rank))).astype(MXU_DTYPE)
    w_up = jnp.pad(w_gate_up, ((0, V7X_LANES - rank), (0, 0))).astype(MXU_DTYPE)
    u, g_low = _odd_in_proj(x, g, w_main, 0, w_low)
    o = _gla_core(u.reshape(bsz, seq, -1), g_low.reshape(bsz, seq, -1), w_up, b_gate_up, head_g)
    return _matmul_residual(x, o.reshape(bsz * seq, -1), w_out, 0, name="odd_out_proj")


def kernel(x, norm_g, ffn_pre_w1, ffn_pre_w3, ffn_pre_w2, ffn_post_w1, ffn_post_w3, ffn_post_w2, even_w_in,
           even_conv_w, even_sinks, even_w_out, odd_w_in, odd_w_gate_up, odd_b_gate_up, odd_head_g, odd_w_out,
           final_g):
    bsz, seq, d = x.shape
    depth = norm_g.shape[0]
    h = x.reshape(bsz * seq, d)
    pre = (ffn_pre_w1, ffn_pre_w3, ffn_pre_w2)
    post = (ffn_post_w1, ffn_post_w3, ffn_post_w2)
    calls = [(stack, layer) for layer in range(depth) for stack in (pre, post)]
    weights = tuple(w[0:1].astype(MXU_DTYPE) for w in pre)
    even_in = even_w_in.astype(MXU_DTYPE)
    for c, (stack, layer) in enumerate(calls):
        last = c == len(calls) - 1
        is_pre = stack is pre
        h, weights = _ffn(h, norm_g[layer, 0 if is_pre else 2], weights, 0, final_g, final_norm=last,
                          cast_next=None if last else calls[c + 1])
        if is_pre:
            i = layer // 2
            if layer % 2 == 0:
                d_main = odd_w_in.shape[2] - odd_w_gate_up.shape[1]
                casts = [(odd_w_in, i, d_main), (odd_w_out, i, d)] if layer + 1 < depth else []
                h, odd_weights = _even_mixer(h, norm_g[layer, 1], even_in, even_conv_w[i], even_sinks[i], even_w_out,
                                             i, bsz, seq, casts)
            else:
                h = _odd_mixer(h, norm_g[layer, 1], odd_w_in, odd_weights[0], odd_w_gate_up[i], odd_b_gate_up[i],
                               odd_head_g[i], odd_weights[1], i, bsz, seq)
    return h.reshape(bsz, seq, d)
```

```python
import functools

import jax
import jax.numpy as jnp
import numpy as np
from jax import lax
from jax.experimental import pallas as pl
from jax.experimental.pallas import tpu as pltpu

F32 = jnp.float32
MXU_DTYPE = jnp.bfloat16

NORM_EPS = 1e-6
FFN_HALF = 0.5
CONV_WIDTH = 3
HEAD_DIM = 64
N_KV_HEADS = 4
WINDOW = 128
ATTN_BLOCK = 128
GLA_HEADS = 4
GLA_CHUNK = 128
GLA_GATE_NORMALIZER = 16.0
ATTN_LOOKAHEAD = 16
PROJECT_BEFORE_STAGING = 1

V7X_LANES = 128
V7X_SUBLANES = 8
BF16_SUBLANES = 16
V7X_MXU_DIM = 256
V7X_VMEM_BYTES = 64 * 1024 * 1024


def _vmem_limit(block_bytes, scratch_bytes, temp_bytes):
    need = 2 * block_bytes + scratch_bytes + temp_bytes
    return int(min(need, V7X_VMEM_BYTES))


def _nbytes(shape, dtype):
    return int(np.prod(shape)) * jnp.dtype(dtype).itemsize


def _tile(n, want):
    if n <= want:
        return n
    t = want
    while n % t:
        t -= V7X_SUBLANES
    return t


def _rms_norm(x, g):
    return x * lax.rsqrt(jnp.mean(x * x, axis=-1, keepdims=True) + NORM_EPS) * g


def _dot_nt(a, b):
    return lax.dot_general(a, b, (((1,), (1,)), ((), ())), preferred_element_type=F32)


def _ffn_body(x_ref, g_ref, w1_ref, w3_ref, w2_ref, fg_ref, *rest, final_norm, n_cast):
    cast_src, o_ref, cast_dst, n_ref = rest[:n_cast], rest[n_cast], rest[n_cast + 1:2 * n_cast + 1], rest[-1]
    k = pl.program_id(1)

    def half_swiglu_chunk():
        for src, dst in zip(cast_src, cast_dst):
            dst[...] = src[...].astype(dst.dtype)
        n = n_ref[...]
        a = jnp.dot(n, w1_ref[...], preferred_element_type=F32)
        b = jnp.dot(n, w3_ref[...], preferred_element_type=F32)
        h = (a * jax.nn.sigmoid(a)) * b
        return FFN_HALF * jnp.dot(h.astype(w2_ref.dtype), w2_ref[...], preferred_element_type=F32)

    @pl.when(k == 0)
    def _():
        n_ref[...] = _rms_norm(x_ref[...], g_ref[...]).astype(n_ref.dtype)
        o_ref[...] = x_ref[...] + half_swiglu_chunk()

    @pl.when(k > 0)
    def _():
        o_ref[...] += half_swiglu_chunk()

    if final_norm:
        @pl.when(k == pl.num_programs(1) - 1)
        def _():
            o_ref[...] = _rms_norm(o_ref[...], fg_ref[...])


def _ffn(x, g, weights, layer, final_g, *, final_norm, cast_next=None):
    w1, w3, w2 = weights
    n_tok, d = x.shape
    d_ff = w1.shape[2]
    tm = _tile(n_tok, 1024)
    tf = _tile(d_ff, 512)
    gi, gk = n_tok // tm, d_ff // tf
    blocks = (2 * _nbytes((tm, d), F32) + 2 * _nbytes((d, tf), MXU_DTYPE) + _nbytes((tf, d), MXU_DTYPE)
              + 2 * _nbytes((1, d), F32))
    scratch = _nbytes((tm, d), MXU_DTYPE)
    temps = 6 * _nbytes((tm, tf), F32)
    in_specs = [
        pl.BlockSpec((tm, d), lambda i, k: (i, 0), pipeline_mode=pl.Buffered(2, use_lookahead=True)),
        pl.BlockSpec((1, d), lambda i, k: (0, 0)),
        pl.BlockSpec((d, tf), lambda i, k: (0, k)),
        pl.BlockSpec((d, tf), lambda i, k: (0, k)),
        pl.BlockSpec((tf, d), lambda i, k: (k, 0)),
        pl.BlockSpec((1, d), lambda i, k: (0, 0)),
    ]
    args = [x, g.reshape(1, d), w1, w3, w2, final_g.reshape(1, d)]
    picks = [None, None, layer, layer, layer, None]
    out_shape = [jax.ShapeDtypeStruct((n_tok, d), F32)]
    out_specs = [pl.BlockSpec((tm, d), lambda i, k: (i, 0))]
    out_picks = [None]
    n_cast = 0
    if cast_next is not None:
        (f1, f3, f2), nxt = cast_next
        rows_in, rows_out = d // gi, d_ff // (gi * gk)
        assert d % gi == 0 and d_ff % (gi * gk) == 0
        assert rows_in % BF16_SUBLANES == 0 and rows_out % BF16_SUBLANES == 0
        n_cast = 3
        args += [f1, f3, f2]
        picks += [nxt, nxt, nxt]
        cast_specs = [
            pl.BlockSpec((rows_in, tf), lambda i, k: (i, k)),
            pl.BlockSpec((rows_in, tf), lambda i, k: (i, k)),
            pl.BlockSpec((rows_out, d), lambda i, k: (i * gk + k, 0)),
        ]
        in_specs += cast_specs
        out_shape += [jax.ShapeDtypeStruct((1, d, d_ff), MXU_DTYPE), jax.ShapeDtypeStruct((1, d, d_ff), MXU_DTYPE),
                      jax.ShapeDtypeStruct((1, d_ff, d), MXU_DTYPE)]
        out_specs += cast_specs
        out_picks += [0, 0, 0]
        blocks += 2 * (_nbytes((rows_in, tf), F32) + _nbytes((rows_in, tf), MXU_DTYPE))
        blocks += _nbytes((rows_out, d), F32) + _nbytes((rows_out, d), MXU_DTYPE)

    def whole(*refs):
        n_in = len(args)
        hbm = [r if p is None else r.at[p] for r, p in zip(refs, picks + out_picks)]
        pltpu.emit_pipeline(
            functools.partial(_ffn_body, final_norm=final_norm, n_cast=n_cast),
            grid=(gi, gk),
            in_specs=in_specs,
            out_specs=out_specs,
        )(*hbm[:n_in], *hbm[n_in:], scratches=(refs[-1],))

    anywhere = pl.BlockSpec(memory_space=pl.ANY)
    out = pl.pallas_call(
        whole,
        out_shape=out_shape,
        in_specs=[anywhere] * len(args),
        out_specs=[anywhere] * len(out_shape),
        scratch_shapes=[pltpu.VMEM((tm, d), MXU_DTYPE)],
        compiler_params=pltpu.CompilerParams(vmem_limit_bytes=_vmem_limit(blocks, scratch, temps)),
        name="ffn_final" if final_norm else "ffn",
    )(*args)
    return out[0], tuple(out[1:])


def _even_in_proj_body(x_ref, g_ref, w_ref, cw_ref, *rest, n_cast, col_tile, tiles_per_seq):
    cast_src = rest[:n_cast]
    yc_ref, uq_ref = rest[n_cast:n_cast + 2]
    cast_dst = rest[n_cast + 2:2 * n_cast + 2]
    n_ref, hs_ref, carry_ref = rest[2 * n_cast + 2:]
    for src, dst in zip(cast_src, cast_dst):
        dst[...] = src[:, 0:dst.shape[1]].astype(dst.dtype)
    tm, d_conv = yc_ref.shape
    halo = carry_ref.shape[0]
    first_tile = lax.rem(pl.program_id(0), tiles_per_seq) == 0

    n_ref[...] = _rms_norm(x_ref[...], g_ref[...]).astype(n_ref.dtype)
    def project(c0, width):
        return jnp.dot(n_ref[...], w_ref[:, c0:c0 + width], preferred_element_type=F32)

    h = project(d_conv, d_conv) * project(2 * d_conv, d_conv)
    hs_ref[0:halo, :] = jnp.where(first_tile, jnp.zeros_like(carry_ref), carry_ref[...])
    hs_ref[halo:, :] = h
    carry_ref[...] = h[tm - halo:, :]
    n_conv = 3 * d_conv
    for c0 in range(0, uq_ref.shape[1], col_tile):
        uq_ref[:, c0:c0 + col_tile] = project(n_conv + c0, col_tile).astype(uq_ref.dtype)
    y = None
    for j in range(CONV_WIDTH):
        off = halo - (CONV_WIDTH - 1) + j
        term = cw_ref[j:j + 1, :] * hs_ref[off:off + tm, :]
        y = term if y is None else y + term
    yc_ref[...] = (project(0, d_conv) * y).astype(yc_ref.dtype)


def _even_in_proj(x, g, w, layer, conv_w, seq, *, casts=()):
    n_tok, d = x.shape
    n_out = w.shape[2]
    d_conv = conv_w.shape[1]
    n_qkv = n_out - 3 * d_conv
    tm = _tile(seq, 512)
    halo = V7X_SUBLANES
    assert halo >= CONV_WIDTH - 1 and tm % halo == 0
    col_tile = _tile(n_qkv, 1536)
    steps = n_tok // tm
    blocks = (_nbytes((tm, d), F32) + _nbytes((1, d), F32) + _nbytes((CONV_WIDTH, d_conv), F32)
              + _nbytes((tm, d_conv), MXU_DTYPE) + _nbytes((tm, n_qkv), MXU_DTYPE))
    scratch = (_nbytes((tm, d), MXU_DTYPE) + _nbytes((d, n_out), MXU_DTYPE) + _nbytes((tm + halo, d_conv), F32)
               + _nbytes((halo, d_conv), F32))
    temps = 5 * _nbytes((tm, d_conv), F32) + _nbytes((tm, col_tile), F32)
    in_specs = [
        pl.BlockSpec((tm, d), lambda i: (i, 0)),
        pl.BlockSpec((1, d), lambda i: (0, 0)),
        pl.BlockSpec((None, d, n_out), lambda i: (layer, 0, 0), pipeline_mode=pl.Buffered(1)),
        pl.BlockSpec((CONV_WIDTH, d_conv), lambda i: (0, 0)),
    ]
    args = [x, g.reshape(1, d), w, conv_w]
    out_shape = [jax.ShapeDtypeStruct((n_tok, d_conv), MXU_DTYPE), jax.ShapeDtypeStruct((n_tok, n_qkv), MXU_DTYPE)]
    out_specs = [pl.BlockSpec((tm, d_conv), lambda i: (i, 0)), pl.BlockSpec((tm, n_qkv), lambda i: (i, 0))]
    for f, nxt, n_cols in casts:
        rows = f.shape[1] // steps
        assert f.shape[1] % steps == 0 and rows % BF16_SUBLANES == 0 and n_cols % V7X_LANES == 0
        args.append(f)
        in_specs.append(pl.BlockSpec((None, rows, f.shape[2]), functools.partial(lambda i, nxt: (nxt, i, 0), nxt=nxt)))
        blocks += _nbytes((rows, f.shape[2]), F32) + _nbytes((rows, n_cols), MXU_DTYPE)
    for f, nxt, n_cols in casts:
        rows = f.shape[1] // steps
        out_shape.append(jax.ShapeDtypeStruct((1, f.shape[1], n_cols), MXU_DTYPE))
        out_specs.append(pl.BlockSpec((None, rows, n_cols), lambda i: (0, i, 0)))
    out = pl.pallas_call(
        functools.partial(_even_in_proj_body, n_cast=len(casts), col_tile=col_tile, tiles_per_seq=seq // tm),
        out_shape=out_shape,
        grid=(steps,),
        in_specs=in_specs,
        out_specs=out_specs,
        scratch_shapes=[pltpu.VMEM((tm, d), MXU_DTYPE), pltpu.VMEM((tm + halo, d_conv), F32),
                        pltpu.VMEM((halo, d_conv), F32)],
        compiler_params=pltpu.CompilerParams(
            dimension_semantics=("arbitrary",),
            vmem_limit_bytes=_vmem_limit(blocks, scratch, temps)),
        name="even_in_proj",
    )(*args)
    return out[0], out[1], list(out[2:])


def _odd_in_proj_body(x_ref, g_ref, w_ref, wt_ref, o_ref, ot_ref, n_ref, *, col_tile):
    n_ref[...] = _rms_norm(x_ref[...], g_ref[...]).astype(n_ref.dtype)
    for c0 in range(0, o_ref.shape[1], col_tile):
        o_ref[:, c0:c0 + col_tile] = jnp.dot(n_ref[...], w_ref[:, c0:c0 + col_tile], preferred_element_type=F32)
    ot_ref[...] = jnp.dot(n_ref[...], wt_ref[...], preferred_element_type=F32)


def _odd_in_proj(x, g, w, layer, w_tail):
    n_tok, d = x.shape
    n_out = w.shape[2]
    n_tail = w_tail.shape[1]
    tm = _tile(n_tok, V7X_MXU_DIM)
    col_tile = _tile(n_out, 1536)
    assert n_out % col_tile == 0
    blocks = (_nbytes((tm, d), F32) + _nbytes((1, d), F32) + _nbytes((tm, n_out), F32)
              + _nbytes((d, n_tail), MXU_DTYPE) + _nbytes((tm, n_tail), F32))
    scratch = _nbytes((tm, d), MXU_DTYPE) + _nbytes((d, n_out), MXU_DTYPE)
    temps = 2 * _nbytes((tm, col_tile), F32)
    return pl.pallas_call(
        functools.partial(_odd_in_proj_body, col_tile=col_tile),
        out_shape=(jax.ShapeDtypeStruct((n_tok, n_out), F32), jax.ShapeDtypeStruct((n_tok, n_tail), F32)),
        grid=(n_tok // tm,),
        in_specs=[
            pl.BlockSpec((tm, d), lambda i: (i, 0)),
            pl.BlockSpec((1, d), lambda i: (0, 0)),
            pl.BlockSpec((None, d, n_out), lambda i: (layer, 0, 0), pipeline_mode=pl.Buffered(1)),
            pl.BlockSpec((d, n_tail), lambda i: (0, 0)),
        ],
        out_specs=(pl.BlockSpec((tm, n_out), lambda i: (i, 0)), pl.BlockSpec((tm, n_tail), lambda i: (i, 0))),
        scratch_shapes=[pltpu.VMEM((tm, d), MXU_DTYPE)],
        compiler_params=pltpu.CompilerParams(
            dimension_semantics=("parallel",),
            vmem_limit_bytes=_vmem_limit(blocks, scratch, temps)),
        name="odd_in_proj",
    )(x, g.reshape(1, d), w, w_tail)


def _matmul_residual_body(x_ref, y_ref, w_ref, o_ref, *, col_tile):
    for c0 in range(0, o_ref.shape[1], col_tile):
        cols = slice(c0, c0 + col_tile)
        o_ref[:, cols] = x_ref[:, cols] + jnp.dot(y_ref[...], w_ref[:, cols], preferred_element_type=F32)


def _matmul_residual(x, y, w, layer, *, name):
    n_tok, d = x.shape
    d_in = y.shape[1]
    tm = _tile(n_tok, 512)
    col_tile = _tile(d, 1024)
    assert d % col_tile == 0
    blocks = 2 * _nbytes((tm, d), F32) + _nbytes((tm, d_in), y.dtype)
    scratch = _nbytes((d_in, d), MXU_DTYPE)
    temps = 2 * _nbytes((tm, col_tile), F32)
    return pl.pallas_call(
        functools.partial(_matmul_residual_body, col_tile=col_tile),
        out_shape=jax.ShapeDtypeStruct((n_tok, d), F32),
        grid=(n_tok // tm,),
        in_specs=[
            pl.BlockSpec((tm, d), lambda i: (i, 0)),
            pl.BlockSpec((tm, d_in), lambda i: (i, 0)),
            pl.BlockSpec((None, d_in, d), lambda i: (layer, 0, 0), pipeline_mode=pl.Buffered(1)),
        ],
        out_specs=pl.BlockSpec((tm, d), lambda i: (i, 0)),
        compiler_params=pltpu.CompilerParams(
            dimension_semantics=("parallel",),
            vmem_limit_bytes=_vmem_limit(blocks, scratch, temps)),
        name=name,
    )(x, y, w)


def _alibi_slopes(n_heads):
    ex = np.arange(1, n_heads + 1, dtype=np.float32) * np.float32(8.0 / n_heads)
    return np.power(np.float32(2.0), -ex)


def _even_core_body(*refs, slopes, tiles_per_seq):
    *io_refs, ya_ref, yb_ref = refs
    step = pl.program_id(0)

    @pl.when(step == 0)
    def _():
        yb_ref[...] = jnp.zeros_like(yb_ref)

    @pl.when(lax.rem(step, 2) == 0)
    def _():
        _even_mix_step(*io_refs, ya_ref, yb_ref, slopes=slopes, tiles_per_seq=tiles_per_seq)

    @pl.when(lax.rem(step, 2) == 1)
    def _():
        _even_mix_step(*io_refs, yb_ref, ya_ref, slopes=slopes, tiles_per_seq=tiles_per_seq)


def _even_mix_step(sinks_ref, yc_ref, q_ref, kv_ref, kvp_ref, x_ref, w_ref, o_ref, ks_ref, vts_ref, bias_ref,
                   y_cur, y_prev, *, slopes, tiles_per_seq):
    tq = yc_ref.shape[1]
    d_conv = yc_ref.shape[2]
    blk = ATTN_BLOCK
    step = pl.program_id(0)
    tile = jnp.minimum(step, pl.num_programs(0) - 2)
    first_tile = lax.rem(tile, tiles_per_seq) == 0
    n_heads = bias_ref.shape[0]

    key = lax.broadcasted_iota(jnp.int32, (2 * blk, blk), 0)
    qry = lax.broadcasted_iota(jnp.int32, (2 * blk, blk), 1)

    @pl.when(step == 0)
    def _():
        dist_i = qry + blk - key
        dist = dist_i.astype(F32)
        valid = (dist_i >= 0) & (dist_i < WINDOW)
        for head in range(n_heads):
            bias_ref[head] = jnp.where(valid, -(float(slopes[head]) * dist), -jnp.inf)

    def project(c):
        cols = slice(c * V7X_MXU_DIM, (c + 1) * V7X_MXU_DIM)
        o_ref[:, cols] = x_ref[:, cols] + jnp.dot(y_prev[...], w_ref[:, cols], preferred_element_type=F32)

    n_slices = o_ref.shape[1] // V7X_MXU_DIM
    for c in range(PROJECT_BEFORE_STAGING):
        project(c)

    half = V7X_LANES // 2
    low = lax.broadcasted_iota(jnp.int32, (tq + blk, V7X_LANES), 1) < half
    top = lax.broadcasted_iota(jnp.int32, (V7X_LANES, tq + blk), 0) < half
    kv_all = jnp.concatenate([kvp_ref[0], kv_ref[0]], axis=0).astype(F32)
    n_slabs = kv_all.shape[1] // (2 * V7X_LANES)
    for s in range(n_slabs):
        slab = kv_all[:, s * V7X_LANES:(s + 1) * V7X_LANES]
        swapped = pltpu.roll(slab, half, axis=1)
        zero = jnp.zeros_like(slab)
        ks_ref[2 * s, 0] = jnp.where(low, slab, zero).astype(ks_ref.dtype)
        ks_ref[2 * s, 1] = jnp.where(low, zero, swapped).astype(ks_ref.dtype)
        ks_ref[2 * s + 1, 0] = jnp.where(low, swapped, zero).astype(ks_ref.dtype)
        ks_ref[2 * s + 1, 1] = jnp.where(low, zero, slab).astype(ks_ref.dtype)
        slab_t = kv_all[:, (n_slabs + s) * V7X_LANES:(n_slabs + s + 1) * V7X_LANES].T
        swapped_t = pltpu.roll(slab_t, half, axis=0)
        zero_t = jnp.zeros_like(slab_t)
        vts_ref[2 * s, 0] = jnp.where(top, slab_t, zero_t).astype(vts_ref.dtype)
        vts_ref[2 * s, 1] = jnp.where(top, zero_t, swapped_t).astype(vts_ref.dtype)
        vts_ref[2 * s + 1, 0] = jnp.where(top, swapped_t, zero_t).astype(vts_ref.dtype)
        vts_ref[2 * s + 1, 1] = jnp.where(top, zero_t, slab_t).astype(vts_ref.dtype)

    no_history = first_tile & (key < blk)
    scale = HEAD_DIM ** -0.5
    group = n_heads // N_KV_HEADS

    def scores(qb, g):
        rows = slice(qb * blk, (qb + 1) * blk)
        keys = slice(qb * blk, (qb + 2) * blk)
        slab0 = g * group // 2
        q_slabs = [q_ref[0, rows, (slab0 + a) * V7X_LANES:(slab0 + a + 1) * V7X_LANES]
                   for a in range(group // 2)]
        q4 = (jnp.concatenate(q_slabs, axis=0) * scale).astype(MXU_DTYPE)
        k_cat = jnp.concatenate([ks_ref[g, 0, keys, :], ks_ref[g, 1, keys, :]], axis=0)
        return _dot_nt(k_cat, q4)

    def attend(qb, g, st):
        rows = slice(qb * blk, (qb + 1) * blk)
        keys = slice(qb * blk, (qb + 2) * blk)
        slab0 = g * group // 2
        p_cols = []
        for a in range(group // 2):
            p_rows = []
            for hi in range(2):
                head = g * group + 2 * a + hi
                sq = st[hi * 2 * blk:(hi + 1) * 2 * blk, a * blk:(a + 1) * blk] + bias_ref[head]
                if qb == 0:
                    sq = jnp.where(no_history, -jnp.inf, sq)
                sink = sinks_ref[head]
                m = jnp.maximum(jnp.max(sq, axis=0, keepdims=True), sink)
                e = jnp.exp(sq - m)
                denom = jnp.sum(e, axis=0, keepdims=True) + jnp.exp(sink - m)
                p_rows.append((e * (1.0 / denom)).astype(MXU_DTYPE))
            p_cols.append(jnp.concatenate(p_rows, axis=0))
        pt = jnp.concatenate(p_cols, axis=1)
        vt_cat = jnp.concatenate([vts_ref[g, 0, :, keys], vts_ref[g, 1, :, keys]], axis=1)
        out = jnp.dot(vt_cat, pt, preferred_element_type=F32).T
        for a in range(group // 2):
            c0 = d_conv + (slab0 + a) * V7X_LANES
            y_cur[rows, c0:c0 + V7X_LANES] = out[a * blk:(a + 1) * blk].astype(y_cur.dtype)

    work = [(qb, g) for qb in range(tq // blk) for g in range(N_KV_HEADS)]
    pending = [scores(*w) for w in work[:ATTN_LOOKAHEAD]]
    y_cur[:, 0:d_conv] = yc_ref[0]
    late = list(range(PROJECT_BEFORE_STAGING, n_slices))
    for i, w in enumerate(work):
        if i + ATTN_LOOKAHEAD < len(work):
            pending.append(scores(*work[i + ATTN_LOOKAHEAD]))
        attend(*w, pending.pop(0))
        if late:
            project(late.pop(0))
    for c in late:
        project(c)


def _even_core(x, y_conv, qkv, sinks, w_out, layer):
    bsz, seq, d_conv = y_conv.shape
    n_tok, d = x.shape
    n_q_heads = sinks.shape[0]
    d_attn = n_q_heads * HEAD_DIM
    d_kv = N_KV_HEADS * HEAD_DIM
    d_mix = d_conv + d_attn
    assert (n_q_heads // N_KV_HEADS) % 2 == 0 and (2 * d_kv) % V7X_LANES == 0 and d_attn % (2 * d_kv) == 0
    tq = _tile(seq, 512)
    assert tq % ATTN_BLOCK == 0 and d % V7X_MXU_DIM == 0
    blocks_per_tile = tq // ATTN_BLOCK
    tiles_per_seq = seq // tq
    n_tiles = bsz * tiles_per_seq
    kv_col = d_attn // (2 * d_kv)
    slopes = _alibi_slopes(n_q_heads)

    def mixed(s):
        t = jnp.minimum(s, n_tiles - 1)
        return t // tiles_per_seq, t % tiles_per_seq

    def cur(col):
        return lambda s: (*mixed(s), col)

    def prev_block(s):
        return mixed(s)[0], jnp.maximum(mixed(s)[1] * blocks_per_tile - 1, 0), kv_col

    def projected(s):
        return jnp.maximum(s - 1, 0), 0

    blocks = (_nbytes((tq, d_conv), y_conv.dtype) + _nbytes((tq, d_attn), qkv.dtype)
              + _nbytes((tq + ATTN_BLOCK, 2 * d_kv), qkv.dtype) + 2 * _nbytes((tq, d), F32))
    scratch = (4 * _nbytes((tq + ATTN_BLOCK, 2 * d_kv), MXU_DTYPE)
               + _nbytes((n_q_heads, ATTN_BLOCK, 2 * ATTN_BLOCK), F32) + 2 * _nbytes((tq, d_mix), MXU_DTYPE)
               + _nbytes((d_mix, d), MXU_DTYPE))
    score_tile = _nbytes((4 * ATTN_BLOCK, 2 * ATTN_BLOCK), F32)
    temps = (2 * _nbytes((tq + ATTN_BLOCK, 2 * d_kv), F32) + (ATTN_LOOKAHEAD + 6) * score_tile
             + 2 * _nbytes((tq, V7X_MXU_DIM), F32))
    return pl.pallas_call(
        functools.partial(_even_core_body, slopes=slopes, tiles_per_seq=tiles_per_seq),
        out_shape=jax.ShapeDtypeStruct((n_tok, d), F32),
        grid=(n_tiles + 1,),
        in_specs=[
            pl.BlockSpec(memory_space=pltpu.SMEM),
            pl.BlockSpec((1, tq, d_conv), cur(0)),
            pl.BlockSpec((1, tq, d_attn), cur(0)),
            pl.BlockSpec((1, tq, 2 * d_kv), cur(kv_col)),
            pl.BlockSpec((1, ATTN_BLOCK, 2 * d_kv), prev_block),
            pl.BlockSpec((tq, d), projected),
            pl.BlockSpec((None, d_mix, d), lambda s: (layer, 0, 0), pipeline_mode=pl.Buffered(1)),
        ],
        out_specs=pl.BlockSpec((tq, d), projected),
        scratch_shapes=[
            pltpu.VMEM((N_KV_HEADS, 2, tq + ATTN_BLOCK, V7X_LANES), MXU_DTYPE),
            pltpu.VMEM((N_KV_HEADS, 2, V7X_LANES, tq + ATTN_BLOCK), MXU_DTYPE),
            pltpu.VMEM((n_q_heads, 2 * ATTN_BLOCK, ATTN_BLOCK), F32),
            pltpu.VMEM((tq, d_mix), MXU_DTYPE),
            pltpu.VMEM((tq, d_mix), MXU_DTYPE),
        ],
        compiler_params=pltpu.CompilerParams(
            dimension_semantics=("arbitrary",),
            vmem_limit_bytes=_vmem_limit(blocks, scratch, temps)),
        name="even_mixer",
    )(sinks, y_conv, qkv, qkv, qkv, x, w_out)


def _split3(x):
    hi = x.astype(MXU_DTYPE)
    r1 = x - hi.astype(F32)
    mid = r1.astype(MXU_DTYPE)
    lo = (r1 - mid.astype(F32)).astype(MXU_DTYPE)
    return hi, mid, lo


GLA_PROLOGUE_STAGES = 3


def _gla_head(load, store, g_low, keep, tri, s_ref):
    q, k, v, r, w_up, b_up, head_g = load()
    tt, hk = q.shape
    ch = GLA_CHUNK
    n_chunks = tt // ch
    sub = keep.shape[0]

    z = jnp.dot(g_low, w_up, preferred_element_type=F32) + b_up
    log_a = (jnp.minimum(z, 0.0) - jnp.log(1.0 + jnp.exp(-jnp.abs(z)))) * (1.0 / GLA_GATE_NORMALIZER)
    parts = jnp.concatenate(_split3(log_a), axis=1)
    yield

    b = []
    for a in range(0, tt, sub):
        c3 = jnp.dot(tri, parts[a:a + sub], preferred_element_type=F32)
        b.append(c3[:, 0:hk] + c3[:, hk:2 * hk] + c3[:, 2 * hk:3 * hk])
    b = jnp.concatenate(b, axis=0)
    decay = jnp.concatenate(
        [jnp.broadcast_to(jnp.exp(b[(c + 1) * ch - 1:(c + 1) * ch, :]), (ch, hk)) for c in range(n_chunks)],
        axis=0)
    grow = jnp.exp(b)
    q_dec = ((q * (hk ** -0.5)) * grow).astype(MXU_DTYPE)
    k_inv_f32 = k * (1.0 / grow)
    k_inv = k_inv_f32.astype(MXU_DTYPE)
    k_tail_t = (k_inv_f32 * decay).T.astype(MXU_DTYPE)
    decay_t = decay.T
    yield

    o_intra = []
    for a in range(0, tt, sub):
        attn = _dot_nt(q_dec[a:a + sub], k_inv[a:a + sub])
        attn = jnp.where(keep, attn, 0.0).astype(MXU_DTYPE)
        o_intra.append(jnp.dot(attn, v[a:a + sub], preferred_element_type=F32))
    o_intra = jnp.concatenate(o_intra, axis=0)
    yield

    o_inter = []
    for c in range(n_chunks):
        rows = slice(c * ch, (c + 1) * ch)
        state = s_ref[...]
        o_inter.append(jnp.dot(q_dec[rows], state.astype(MXU_DTYPE), preferred_element_type=F32))
        kv = jnp.dot(k_tail_t[:, rows], v[rows], preferred_element_type=F32)
        s_ref[...] = decay_t[:, c * ch:c * ch + 1] * state + kv
        yield
    o = o_intra + jnp.concatenate(o_inter, axis=0)

    o = _rms_norm(o, head_g)
    half_r = 0.5 * r
    store(o * (half_r + half_r * jnp.tanh(half_r)))


def _gla_body(q_ref, k_ref, v_ref, r_ref, gl_ref, wgu_ref, bgu_ref, hg_ref, o_ref, s_ref):
    tt = q_ref.shape[1]
    n_heads = s_ref.shape[0]
    hk = q_ref.shape[2] // n_heads
    hv = v_ref.shape[2] // n_heads
    ch = GLA_CHUNK
    sub = min(tt, V7X_MXU_DIM)

    @pl.when(pl.program_id(1) == 0)
    def _():
        s_ref[...] = jnp.zeros_like(s_ref)

    ri = lax.broadcasted_iota(jnp.int32, (sub, sub), 0)
    ci = lax.broadcasted_iota(jnp.int32, (sub, sub), 1)
    keep = (ri // ch == ci // ch) & (ci <= ri)
    tri = jnp.where(keep, 1.0, 0.0).astype(MXU_DTYPE)

    g_low = gl_ref[0].astype(MXU_DTYPE)

    def head(h):
        ks = slice(h * hk, (h + 1) * hk)
        vs = slice(h * hv, (h + 1) * hv)

        def load():
            return (q_ref[0, :, ks], k_ref[0, :, ks], v_ref[0, :, vs].astype(MXU_DTYPE), r_ref[0, :, vs],
                    wgu_ref[:, ks], bgu_ref[:, ks], hg_ref[...])

        def store(o):
            o_ref[0, :, vs] = o.astype(o_ref.dtype)

        return _gla_head(load, store, g_low, keep, tri, s_ref.at[h])

    heads = [head(h) for h in range(n_heads)]
    finished = object()
    for _ in range(GLA_PROLOGUE_STAGES):
        next(heads[0])
    for h in range(n_heads):
        ahead = heads[h + 1] if h + 1 < n_heads else None
        todo = GLA_PROLOGUE_STAGES if ahead is not None else 0
        for _ in range(tt // ch):
            next(heads[h])
            if todo:
                next(ahead)
                todo -= 1
        assert next(heads[h], finished) is finished
        for _ in range(todo):
            next(ahead)


def _gla_core(u, g_low, w_gate_up, b_gate_up, head_g):
    bsz, seq, _ = u.shape
    gl_w, d_k = w_gate_up.shape
    hk = d_k // GLA_HEADS
    hv = head_g.shape[0]
    d_v = hv * GLA_HEADS
    assert d_v % d_k == 0
    tt = _tile(seq, 512)
    assert GLA_CHUNK % V7X_LANES == 0 and tt % GLA_CHUNK == 0 and min(tt, V7X_MXU_DIM) % GLA_CHUNK == 0
    blocks = (2 * _nbytes((tt, d_k), F32) + 2 * _nbytes((tt, d_v), F32) + _nbytes((tt, gl_w), F32)
              + _nbytes((gl_w, d_k), MXU_DTYPE) + _nbytes((1, d_k), F32) + _nbytes((1, hv), F32)
              + _nbytes((tt, d_v), MXU_DTYPE))
    scratch = _nbytes((GLA_HEADS, hk, hv), F32)
    temps = GLA_HEADS * (8 * _nbytes((tt, hk), F32) + 4 * _nbytes((tt, hv), F32))
    return pl.pallas_call(
        _gla_body,
        out_shape=jax.ShapeDtypeStruct((bsz, seq, d_v), MXU_DTYPE),
        grid=(bsz, seq // tt),
        in_specs=[
            pl.BlockSpec((1, tt, d_k), lambda b, t: (b, t, 0)),
            pl.BlockSpec((1, tt, d_k), lambda b, t: (b, t, 1)),
            pl.BlockSpec((1, tt, d_v), lambda b, t: (b, t, 2 * d_k // d_v)),
            pl.BlockSpec((1, tt, d_v), lambda b, t: (b, t, 2 * d_k // d_v + 1)),
            pl.BlockSpec((1, tt, gl_w), lambda b, t: (b, t, 0)),
            pl.BlockSpec((gl_w, d_k), lambda b, t: (0, 0)),
            pl.BlockSpec((1, d_k), lambda b, t: (0, 0)),
            pl.BlockSpec((1, hv), lambda b, t: (0, 0)),
        ],
        out_specs=pl.BlockSpec((1, tt, d_v), lambda b, t: (b, t, 0)),
        scratch_shapes=[pltpu.VMEM((GLA_HEADS, hk, hv), F32)],
        compiler_params=pltpu.CompilerParams(
            dimension_semantics=("parallel", "arbitrary"),
            vmem_limit_bytes=_vmem_limit(blocks, scratch, temps)),
        name="gla_core",
    )(u, u, u, u, g_low, w_gate_up, b_gate_up.reshape(1, d_k), head_g.reshape(1, hv))


def _even_mixer(x, g, w_in, conv_w, sinks, w_out, i, bsz, seq, casts):
    y_conv, qkv, converted = _even_in_proj(x, g, w_in, i, conv_w, seq,
                                           casts=[(w_out, i, w_out.shape[2])] + list(casts))
    h = _even_core(x, y_conv.reshape(bsz, seq, -1), qkv.reshape(bsz, seq, -1), sinks, converted[0], 0)
    return h, converted[1:]


def _odd_mixer(x, g, w_in, w_main, w_gate_up, b_gate_up, head_g, w_out, i, bsz, seq):
    rank, d_k = w_gate_up.shape
    d_main = w_in.shape[2] - rank
    w_low = jnp.pad(w_in[i, :, d_main:], ((0, 0), (0, V7X_LANES - rank))).astype(MXU_DTYPE)
    w_up = jnp.pad(w_gate_up, ((0, V7X_LANES - rank), (0, 0))).astype(MXU_DTYPE)
    u, g_low = _odd_in_proj(x, g, w_main, 0, w_low)
    o = _gla_core(u.reshape(bsz, seq, -1), g_low.reshape(bsz, seq, -1), w_up, b_gate_up, head_g)
    return _matmul_residual(x, o.reshape(bsz * seq, -1), w_out, 0, name="odd_out_proj")


def kernel(x, norm_g, ffn_pre_w1, ffn_pre_w3, ffn_pre_w2, ffn_post_w1, ffn_post_w3, ffn_post_w2, even_w_in,
           even_conv_w, even_sinks, even_w_out, odd_w_in, odd_w_gate_up, odd_b_gate_up, odd_head_g, odd_w_out,
           final_g):
    bsz, seq, d = x.shape
    depth = norm_g.shape[0]
    h = x.reshape(bsz * seq, d)
    pre = (ffn_pre_w1, ffn_pre_w3, ffn_pre_w2)
    post = (ffn_post_w1, ffn_post_w3, ffn_post_w2)
    calls = [(stack, layer) for layer in range(depth) for stack in (pre, post)]
    weights = tuple(w[0:1].astype(MXU_DTYPE) for w in pre)
    even_in = even_w_in.astype(MXU_DTYPE)
    for c, (stack, layer) in enumerate(calls):
        last = c == len(calls) - 1
        is_pre = stack is pre
        h, weights = _ffn(h, norm_g[layer, 0 if is_pre else 2], weights, 0, final_g, final_norm=last,
                          cast_next=None if last else calls[c + 1])
        if is_pre:
            i = layer // 2
            if layer % 2 == 0:
                d_main = odd_w_in.shape[2] - odd_w_gate_up.shape[1]
                casts = [(odd_w_in, i, d_main), (odd_w_out, i, d)] if layer + 1 < depth else []
                h, odd_weights = _even_mixer(h, norm_g[layer, 1], even_in, even_conv_w[i], even_sinks[i], even_w_out,
                                             i, bsz, seq, casts)
            else:
                h = _odd_mixer(h, norm_g[layer, 1], odd_w_in, odd_weights[0], odd_w_gate_up[i], odd_b_gate_up[i],
                               odd_head_g[i], odd_weights[1], i, bsz, seq)
    return h.reshape(bsz, seq, d)
```
